```python
import math
import jax
import jax.numpy as jnp
from jax import lax
import numpy as np

D_MODEL = 1024
BATCH = 4
SEQ = 4096
DEPTH = 4
DEC_BATCH = 32
DEC_SEQ = 1
PAST_LEN = 8192
PAGE_SIZE = 128

EPS = 1e-6
ROPE_THETA = 10000.0
BRANCH_W = 512
N_BRANCH = 3
DA_HEADS = 4
DA_QK = 64
DA_V = 2 * DA_QK
DA_QBLOCK = 128
ML_HEADS = 4
ML_QK = 64
ML_V = 128
ML_CHUNK = 64
NSA_HEADS = 8
NSA_GROUPS = 2
NSA_HPG = NSA_HEADS // NSA_GROUPS
NSA_DH = 64
NSA_CMP_LEN = 32
NSA_CMP_STRIDE = 16
NSA_SEL_LEN = 64
NSA_TOPK = 16
NSA_WINDOW = 512
NSA_QBLOCK = 64
FORCE_SCORE = 1.0e4
N_EXPERTS = 32
TOP_K = 4
D_FF = 1024
SWIGLU_LIMIT = 7.0
SWIGLU_ALPHA = 1.702
IN_SPLITS = (DA_HEADS * 2 * DA_QK, DA_HEADS * 2 * DA_QK, DA_HEADS * DA_V,
             ML_HEADS * ML_QK, ML_HEADS * ML_QK, ML_HEADS * ML_V, ML_HEADS * ML_V, ML_HEADS, ML_HEADS,
             NSA_HEADS * NSA_DH, 6 * NSA_GROUPS * NSA_DH, 3 * NSA_HEADS)
D_IN = sum(IN_SPLITS)

kernel_name = "hybrid_diffattn_mlstm_nsa_moe_step"


def rms_norm(x, g):
    xf = x.astype(jnp.float32)
    y = xf * lax.rsqrt(jnp.mean(xf * xf, axis=-1, keepdims=True) + EPS)
    return (y * g.astype(jnp.float32)).astype(x.dtype)


def rope(x, pos):
    half = x.shape[-1] // 2
    inv = ROPE_THETA ** (-jnp.arange(half, dtype=jnp.float32) / half)
    ang = pos.astype(jnp.float32)[:, None] * inv[None, :]
    ang = ang.reshape((pos.shape[0],) + (1,) * (x.ndim - 3) + (half,))
    cos, sin = jnp.cos(ang), jnp.sin(ang)
    xf = x.astype(jnp.float32)
    x1, x2 = xf[..., :half], xf[..., half:]
    return jnp.concatenate([x1 * cos - x2 * sin, x2 * cos + x1 * sin], axis=-1).astype(x.dtype)


def masked_softmax(s, mask):
    s = jnp.where(mask, s.astype(jnp.float32), -1e30)
    m = jnp.max(s, axis=-1, keepdims=True)
    e = jnp.where(mask, jnp.exp(s - m), 0.0)
    return e / jnp.maximum(jnp.sum(e, axis=-1, keepdims=True), 1e-30)


def sweep_queries(fn, qs, qpos, block):
    B, Q = qs[0].shape[:2]
    nb = Q // block

    def split(a):
        return jnp.moveaxis(a.reshape((B, nb, block) + a.shape[2:]), 1, 0)

    out = lax.map(lambda xs: fn(*xs[0], xs[1]), (tuple(split(a) for a in qs), qpos.reshape(nb, block)))
    out = jnp.moveaxis(out, 0, 1)
    return out.reshape((B, Q) + out.shape[3:])


def gather_pages(cache, l, page_table):
    g = cache[l, page_table]
    return g.reshape((g.shape[0], g.shape[1] * g.shape[2]) + g.shape[3:])


def diff_attn_core(q, k, v, qpos, kpos, lam, lam_init, g_sub):
    s = jnp.einsum('bqhcd,bkhcd->bhcqk', q, k) * (DA_QK ** -0.5)
    mask = kpos[None, :] <= qpos[:, None]
    p = masked_softmax(s, mask)
    a = p[:, :, 0] - lam * p[:, :, 1]
    o = jnp.einsum('bhqk,bkhd->bqhd', a.astype(v.dtype), v)
    return rms_norm(o, g_sub) * (1.0 - lam_init)


def mlstm_chunked(q, k, v, log_i, log_f, C0, n0, m0):
    B, S, H, DK = q.shape
    L = math.gcd(S, ML_CHUNK)
    nc = S // L

    def to_chunks(a):
        return jnp.moveaxis(a.reshape((B, nc, L) + a.shape[2:]), 1, 0)

    qf = q.astype(jnp.float32)
    kf = k.astype(jnp.float32) * (DK ** -0.5)
    vf = v.astype(jnp.float32)
    causal = jnp.tril(jnp.ones((L, L), dtype=bool))[None, :, :, None]

    def step(carry, xs):
        C, n, m = carry
        qc, kc, vc, ic, fc = xs
        b = jnp.cumsum(fc, axis=1)
        a = b + m[:, None, :]
        D = jnp.where(causal, b[:, :, None, :] - b[:, None, :, :] + ic[:, None, :, :], -jnp.inf)
        mt = jnp.maximum(a, jnp.max(D, axis=2))
        w_inter = jnp.exp(a - mt)
        w_intra = jnp.exp(D - mt[:, :, None, :])
        sc = w_intra * jnp.einsum('bthd,bshd->btsh', qc, kc)
        num = w_inter[..., None] * jnp.einsum('bhvd,bthd->bthv', C, qc) + jnp.einsum('btsh,bshv->bthv', sc, vc)
        den = w_inter * jnp.einsum('bhd,bthd->bth', n, qc) + jnp.sum(sc, axis=2)
        h = num / jnp.maximum(jnp.abs(den), jnp.exp(-mt))[..., None]
        m_new = mt[:, -1]
        wi = w_intra[:, -1]
        wc = w_inter[:, -1]
        C_new = wc[..., None, None] * C + jnp.einsum('bsh,bshv,bshd->bhvd', wi, vc, kc)
        n_new = wc[..., None] * n + jnp.einsum('bsh,bshd->bhd', wi, kc)
        return (C_new, n_new, m_new), h

    (C1, n1, m1), hs = lax.scan(step, (C0, n0, m0),
                                (to_chunks(qf), to_chunks(kf), to_chunks(vf), to_chunks(log_i), to_chunks(log_f)))
    hs = jnp.moveaxis(hs, 0, 1).reshape(B, S, H, v.shape[-1])
    return hs, C1, n1, m1


def nsa_compress(rows, pos_emb, w):
    B, T, G, dh = rows.shape
    Tp = -(-T // NSA_SEL_LEN) * NSA_SEL_LEN
    rows = jnp.pad(rows, ((0, 0), (0, Tp - T), (0, 0), (0, 0)))
    ch = rows.reshape(B, Tp // NSA_CMP_STRIDE, NSA_CMP_STRIDE, G, dh)
    n_sub = NSA_CMP_LEN // NSA_CMP_STRIDE
    nc = ch.shape[1] - n_sub + 1
    blocks = jnp.concatenate([ch[:, i:i + nc] for i in range(n_sub)], axis=2)
    blocks = blocks + pos_emb[None, None, :, None, :]
    return jnp.einsum('bnlgd,lde->bnge', blocks, w.reshape(NSA_CMP_LEN, dh, dh))


def nsa_cmp_branch(q, kc, vc, qpos):
    B, Q, H, dh = q.shape
    nc = kc.shape[1]
    qg = q.reshape(B, Q, NSA_GROUPS, NSA_HPG, dh)
    s = jnp.einsum('bqgjd,bngd->bgjqn', qg, kc) * (dh ** -0.5)
    ends = jnp.arange(nc) * NSA_CMP_STRIDE + NSA_CMP_LEN - 1
    mask = ends[None, :] <= qpos[:, None]
    p = masked_softmax(s, mask)
    o = jnp.einsum('bgjqn,bngd->bqgjd', p.astype(vc.dtype), vc).reshape(B, Q, H, dh)
    return o, jnp.sum(p, axis=2)


def nsa_select(imp_cmp, qpos, nsb):
    r = NSA_SEL_LEN // NSA_CMP_STRIDE
    imp = jnp.pad(imp_cmp, ((0, 0), (0, 0), (0, 0), (0, nsb * r - imp_cmp.shape[-1])))
    imp = jnp.sum(imp.reshape(imp.shape[:3] + (nsb, r)), axis=-1)
    j = jnp.arange(nsb)[None, :]
    qb = (qpos // NSA_SEL_LEN)[:, None]
    forced = (j == 0) | (j == qb) | (j == qb - 1)
    score = jnp.where(forced, FORCE_SCORE, imp)
    score = jnp.where(j <= qb, score, -FORCE_SCORE)
    _, idx = lax.top_k(score, min(NSA_TOPK, nsb))
    valid = idx <= (qpos // NSA_SEL_LEN)[None, None, :, None]
    return jnp.transpose(idx, (0, 2, 1, 3)), jnp.transpose(valid, (0, 2, 1, 3))


def to_sel_blocks(rows, nsb):
    B, T, G, dh = rows.shape
    rows = jnp.pad(rows, ((0, 0), (0, nsb * NSA_SEL_LEN - T), (0, 0), (0, 0)))
    return jnp.transpose(rows.reshape(B, nsb, NSA_SEL_LEN, G, dh), (0, 3, 1, 2, 4))


def nsa_slc_branch(q, idx, valid, qpos, kblk, vblk):
    B, Qb, H, dh = q.shape
    bi = jnp.arange(B)[:, None, None, None]
    gi = jnp.arange(NSA_GROUPS)[None, None, :, None]
    ks = kblk[bi, gi, idx]
    vs = vblk[bi, gi, idx]
    kpos = idx[..., None] * NSA_SEL_LEN + jnp.arange(NSA_SEL_LEN)
    mask = valid[..., None] & (kpos <= qpos[None, :, None, None, None])
    qg = q.reshape(B, Qb, NSA_GROUPS, NSA_HPG, dh)
    s = jnp.einsum('bqgjd,bqgnld->bqgjnl', qg, ks) * (dh ** -0.5)
    p = masked_softmax(s.reshape(B, Qb, NSA_GROUPS, NSA_HPG, -1), mask.reshape(B, Qb, NSA_GROUPS, 1, -1))
    o = jnp.einsum('bqgjm,bqgmd->bqgjd', p.astype(vs.dtype), vs.reshape(B, Qb, NSA_GROUPS, -1, dh))
    return o.reshape(B, Qb, H, dh)


def nsa_win_branch(q, kw, vw, qpos, kpos):
    B, NB, QB, H, dh = q.shape
    qg = q.reshape(B, NB, QB, NSA_GROUPS, NSA_HPG, dh)
    s = jnp.einsum('bnqgjd,bnkgd->bngjqk', qg, kw) * (dh ** -0.5)
    dpos = qpos[:, :, None] - kpos[:, None, :]
    mask = (dpos >= 0) & (dpos < NSA_WINDOW) & (kpos[:, None, :] >= 0)
    p = masked_softmax(s, mask[None, :, None, None])
    o = jnp.einsum('bngjqk,bnkgd->bnqgjd', p.astype(vw.dtype), vw)
    return o.reshape(B, NB * QB, H, dh)


def token_mixers(h, pos, l, lp, past):
    B, Q, D = h.shape
    f32 = jnp.float32
    u = h @ lp['w_in']
    cuts = [int(c) for c in np.cumsum(IN_SPLITS)[:-1]]
    qa, ka, va, qm, km, vm, om, im, fm, qn, kvn, gn = jnp.split(u, cuts, axis=-1)

    qa = rope(qa.reshape(B, Q, DA_HEADS, 2, DA_QK), pos)
    ka = rope(ka.reshape(B, Q, DA_HEADS, 2, DA_QK), pos)
    va = va.reshape(B, Q, DA_HEADS, DA_V)
    new_diff = jnp.stack([ka.reshape(B, Q, DA_HEADS, DA_V), va], axis=2)
    if past is None:
        k_all, v_all = ka, va
    else:
        pd = past['diff']
        k_all = jnp.concatenate([pd[:, :, 0].reshape(B, -1, DA_HEADS, 2, DA_QK), ka], axis=1)
        v_all = jnp.concatenate([pd[:, :, 1], va], axis=1)
    kpos = jnp.arange(k_all.shape[1])
    lam_init = 0.8 - 0.6 * math.exp(-0.3 * l)
    lv = lp['da_lam'].astype(f32)
    lam = jnp.exp(jnp.sum(lv[0] * lv[1])) - jnp.exp(jnp.sum(lv[2] * lv[3])) + lam_init
    g_sub = lp['da_subln_g']
    ya = sweep_queries(lambda q, p: diff_attn_core(q, k_all, v_all, p, kpos, lam, lam_init, g_sub),
                       (qa,), pos, math.gcd(Q, DA_QBLOCK))
    ya = ya.reshape(B, Q, DA_HEADS * DA_V)

    qm = qm.reshape(B, Q, ML_HEADS, ML_QK)
    km = km.reshape(B, Q, ML_HEADS, ML_QK)
    vm = vm.reshape(B, Q, ML_HEADS, ML_V)
    log_i = (im + lp['ml_gate_b'][:ML_HEADS]).astype(f32)
    log_f = jax.nn.log_sigmoid((fm + lp['ml_gate_b'][ML_HEADS:]).astype(f32))
    if past is None:
        C0 = jnp.zeros((B, ML_HEADS, ML_V, ML_QK), f32)
        n0 = jnp.zeros((B, ML_HEADS, ML_QK), f32)
        m0 = jnp.zeros((B, ML_HEADS), f32)
    else:
        C0, n0, m0 = past['C'].astype(f32), past['n'].astype(f32), past['m'].astype(f32)
    hm, C1, n1, m1 = mlstm_chunked(qm, km, vm, log_i, log_f, C0, n0, m0)
    hm = rms_norm(hm.astype(h.dtype), lp['ml_norm_g']) * jax.nn.sigmoid(om.reshape(B, Q, ML_HEADS, ML_V))
    yb = hm.reshape(B, Q, ML_HEADS * ML_V)

    qn = qn.reshape(B, Q, NSA_HEADS, NSA_DH)
    kvn = kvn.reshape(B, Q, 6, NSA_GROUPS, NSA_DH)
    cmp_k, cmp_v = kvn[:, :, 0], kvn[:, :, 1]
    slc_k, slc_v = rope(kvn[:, :, 2], pos), kvn[:, :, 3]
    win_k, win_v = rope(kvn[:, :, 4], pos), kvn[:, :, 5]
    qn_rot = rope(qn, pos)
    new_nsa = jnp.stack([cmp_k, cmp_v, slc_k, slc_v], axis=2)
    if past is not None:
        pn = past['nsa']
        cmp_k = jnp.concatenate([pn[:, :, 0], cmp_k], axis=1)
        cmp_v = jnp.concatenate([pn[:, :, 1], cmp_v], axis=1)
        slc_k = jnp.concatenate([pn[:, :, 2], slc_k], axis=1)
        slc_v = jnp.concatenate([pn[:, :, 3], slc_v], axis=1)
    nsb = -(-cmp_k.shape[1] // NSA_SEL_LEN)
    kc = nsa_compress(cmp_k, lp['nsa_cmp_pos'][0], lp['nsa_cmp_w'][0])
    vc = nsa_compress(cmp_v, lp['nsa_cmp_pos'][1], lp['nsa_cmp_w'][1])
    o_cmp, imp = nsa_cmp_branch(qn, kc, vc, pos)
    idx, valid = nsa_select(imp, pos, nsb)
    kblk = to_sel_blocks(slc_k, nsb)
    vblk = to_sel_blocks(slc_v, nsb)
    o_slc = sweep_queries(lambda q, ix, ok, p: nsa_slc_branch(q, ix, ok, p, kblk, vblk),
                          (qn_rot, idx, valid), pos, math.gcd(Q, NSA_QBLOCK))
    if past is None:
        win_len = min(NSA_WINDOW, Q)
        QB = math.gcd(Q, NSA_QBLOCK)
        nb, nw = Q // QB, NSA_WINDOW // QB
        kp = jnp.pad(win_k, ((0, 0), (NSA_WINDOW, 0), (0, 0), (0, 0))).reshape(B, nb + nw, QB, NSA_GROUPS, NSA_DH)
        vp = jnp.pad(win_v, ((0, 0), (NSA_WINDOW, 0), (0, 0), (0, 0))).reshape(B, nb + nw, QB, NSA_GROUPS, NSA_DH)
        band_k = jnp.concatenate([kp[:, i:i + nb] for i in range(nw + 1)], axis=2)
        band_v = jnp.concatenate([vp[:, i:i + nb] for i in range(nw + 1)], axis=2)
        kpos_w = (jnp.arange(nb) * QB)[:, None] - NSA_WINDOW + jnp.arange((nw + 1) * QB)[None, :]
        o_win = nsa_win_branch(qn_rot.reshape(B, nb, QB, NSA_HEADS, NSA_DH), band_k, band_v,
                               pos.reshape(nb, QB), kpos_w)
        new_win = jnp.stack([win_k, win_v], axis=2)[:, Q - win_len:]
    else:
        buf = past['win']
        win_len = buf.shape[1]
        kw = jnp.concatenate([buf[:, :, 0], win_k], axis=1)
        vw = jnp.concatenate([buf[:, :, 1], win_v], axis=1)
        kpos_w = (pos[0] - win_len) + jnp.arange(win_len + Q)
        o_win = nsa_win_branch(qn_rot[:, None], kw[:, None], vw[:, None], pos[None, :], kpos_w[None, :])
        new_win = jnp.stack([kw, vw], axis=2)[:, -win_len:]
    gn = jax.nn.sigmoid(gn.reshape(B, Q, NSA_HEADS, 3))
    yc = (gn[..., 0:1] * o_cmp + gn[..., 1:2] * o_slc + gn[..., 2:3] * o_win).reshape(B, Q, NSA_HEADS * NSA_DH)

    ys = jnp.stack([ya, yb, yc], axis=2)
    proj = jnp.einsum('bqnw,nwd->bqnd', ys, lp['w_branch'])
    gates = jax.nn.sigmoid(h @ lp['w_bgate'] + lp['b_bgate']).reshape(B, Q, N_BRANCH, D)
    y = jnp.sum(gates * proj, axis=2) @ lp['w_out']
    state = {'diff': new_diff, 'nsa': new_nsa, 'win': new_win, 'C': C1, 'n': n1, 'm': m1}
    return y, state


def moe(h, w_router, b_router, w_up, b_up, w_down, b_down):
    B, Q, D = h.shape
    t = h.reshape(B * Q, D)
    logits = (t @ w_router + b_router).astype(jnp.float32)
    top_v, top_i = lax.top_k(logits, TOP_K)
    wts = jax.nn.softmax(top_v, axis=-1)
    gate = jnp.einsum('tk,tke->et', wts, jax.nn.one_hot(top_i, N_EXPERTS, dtype=jnp.float32))

    def expert(acc, xs):
        wu, bu, wd, bd, g = xs
        uu = t @ wu + bu
        gl = jnp.minimum(uu[:, :D_FF], SWIGLU_LIMIT)
        up = jnp.clip(uu[:, D_FF:], -SWIGLU_LIMIT, SWIGLU_LIMIT)
        act = gl * jax.nn.sigmoid(SWIGLU_ALPHA * gl) * (up + 1.0)
        return acc + g[:, None].astype(t.dtype) * (act @ wd + bd), None

    y, _ = lax.scan(expert, jnp.zeros_like(t), (w_up, b_up, w_down, b_down, gate))
    return y.reshape(B, Q, D)


def decoder_layer(x, c, pos, l, lp, past):
    mod = (jax.nn.silu(c) @ lp['w_ada'] + lp['b_ada'])[:, None, :]
    sh1, sc1, g1, sh2, sc2, g2 = jnp.split(mod, 6, axis=-1)
    h = rms_norm(x, lp['norm_mix_g']) * (1.0 + sc1) + sh1
    y, state = token_mixers(h, pos, l, lp, past)
    x = x + g1 * y
    h = rms_norm(x, lp['norm_ffn_g']) * (1.0 + sc2) + sh2
    x = x + g2 * moe(h, lp['w_router'], lp['b_router'], lp['w_up'], lp['b_up'], lp['w_down'], lp['b_down'])
    return x, state


def setup_inputs(seed: int = 0) -> dict:
    key = jax.random.key(seed)
    ks = list(jax.random.split(key, 40))
    f32 = jnp.float32

    def nrm(i, shape, scale):
        return jax.random.normal(ks[i], shape, f32) * scale

    D = D_MODEL
    n_pages = PAST_LEN // PAGE_SIZE
    used = DEC_BATCH * n_pages
    n_pool = used + max(1, used // 4)
    win_len = min(NSA_WINDOW, PAST_LEN)
    page_table = jax.random.permutation(ks[0], n_pool)[:used].reshape(DEC_BATCH, n_pages).astype(jnp.int32)
    return {
        'x_prompt': nrm(1, (BATCH, SEQ, D), 1.0),
        'x_sample': nrm(2, (DEC_BATCH, DEC_SEQ, D), 1.0),
        'cache_diff_kv': nrm(3, (DEPTH, n_pool, PAGE_SIZE, 2, DA_HEADS, DA_V), 1.0),
        'cache_nsa_kv': nrm(4, (DEPTH, n_pool, PAGE_SIZE, 4, NSA_GROUPS, NSA_DH), 1.0),
        'state_nsa_win': nrm(5, (DEPTH, DEC_BATCH, win_len, 2, NSA_GROUPS, NSA_DH), 1.0),
        'state_mlstm_C': nrm(6, (DEPTH, DEC_BATCH, ML_HEADS, ML_V, ML_QK), 0.5),
        'state_mlstm_n': nrm(7, (DEPTH, DEC_BATCH, ML_HEADS, ML_QK), 0.5),
        'state_mlstm_m': nrm(8, (DEPTH, DEC_BATCH, ML_HEADS), 1.0),
        'page_table': page_table,
        'c_prompt': nrm(9, (BATCH, D), 1.0),
        'c_sample': nrm(10, (DEC_BATCH, D), 1.0),
        'norm_mix_g': 1.0 + nrm(11, (DEPTH, D), 0.05),
        'norm_ffn_g': 1.0 + nrm(12, (DEPTH, D), 0.05),
        'w_ada': nrm(13, (DEPTH, D, 6 * D), 0.5 * D ** -0.5),
        'b_ada': nrm(14, (DEPTH, 6 * D), 0.02),
        'w_in': nrm(15, (DEPTH, D, D_IN), D ** -0.5),
        'da_lam': nrm(16, (DEPTH, 4, DA_QK), 0.1),
        'da_subln_g': 1.0 + nrm(17, (DEPTH, DA_V), 0.05),
        'ml_gate_b': jnp.concatenate([nrm(18, (DEPTH, ML_HEADS), 0.1), 3.0 + nrm(19, (DEPTH, ML_HEADS), 0.5)], axis=-1),
        'ml_norm_g': 1.0 + nrm(20, (DEPTH, ML_V), 0.05),
        'nsa_cmp_pos': nrm(21, (DEPTH, 2, NSA_CMP_LEN, NSA_DH), 0.1),
        'nsa_cmp_w': nrm(22, (DEPTH, 2, NSA_CMP_LEN * NSA_DH, NSA_DH), (NSA_CMP_LEN * NSA_DH) ** -0.5),
        'w_branch': nrm(23, (DEPTH, N_BRANCH, BRANCH_W, D), BRANCH_W ** -0.5),
        'w_bgate': nrm(24, (DEPTH, D, N_BRANCH * D), D ** -0.5),
        'b_bgate': nrm(25, (DEPTH, N_BRANCH * D), 0.02),
        'w_out': nrm(26, (DEPTH, D, D), D ** -0.5),
        'w_router': nrm(27, (DEPTH, D, N_EXPERTS), D ** -0.5),
        'b_router': nrm(28, (DEPTH, N_EXPERTS), 0.01),
        'w_up': nrm(29, (DEPTH, N_EXPERTS, D, 2 * D_FF), D ** -0.5),
        'b_up': nrm(30, (DEPTH, N_EXPERTS, 2 * D_FF), 0.01),
        'w_down': nrm(31, (DEPTH, N_EXPERTS, D_FF, D), D_FF ** -0.5),
        'b_down': nrm(32, (DEPTH, N_EXPERTS, D), 0.01),
        'final_g': 1.0 + nrm(33, (D,), 0.05),
    }


def reference(x_prompt, x_sample, cache_diff_kv, cache_nsa_kv, state_nsa_win, state_mlstm_C, state_mlstm_n,
              state_mlstm_m, page_table, c_prompt, c_sample, norm_mix_g, norm_ffn_g, w_ada, b_ada, w_in, da_lam,
              da_subln_g, ml_gate_b, ml_norm_g, nsa_cmp_pos, nsa_cmp_w, w_branch, w_bgate, b_bgate, w_out,
              w_router, b_router, w_up, b_up, w_down, b_down, final_g):
    pos_p = jnp.arange(x_prompt.shape[1])
    pos_s = PAST_LEN + jnp.arange(x_sample.shape[1])
    xp, xs = x_prompt, x_sample
    sp, ss = [], []
    for l in range(DEPTH):
        lp = {'norm_mix_g': norm_mix_g[l], 'norm_ffn_g': norm_ffn_g[l], 'w_ada': w_ada[l], 'b_ada': b_ada[l],
              'w_in': w_in[l], 'da_lam': da_lam[l], 'da_subln_g': da_subln_g[l], 'ml_gate_b': ml_gate_b[l],
              'ml_norm_g': ml_norm_g[l], 'nsa_cmp_pos': nsa_cmp_pos[l], 'nsa_cmp_w': nsa_cmp_w[l],
              'w_branch': w_branch[l], 'w_bgate': w_bgate[l], 'b_bgate': b_bgate[l], 'w_out': w_out[l],
              'w_router': w_router[l], 'b_router': b_router[l], 'w_up': w_up[l], 'b_up': b_up[l],
              'w_down': w_down[l], 'b_down': b_down[l]}
        past = {'diff': gather_pages(cache_diff_kv, l, page_table), 'nsa': gather_pages(cache_nsa_kv, l, page_table),
                'win': state_nsa_win[l], 'C': state_mlstm_C[l], 'n': state_mlstm_n[l], 'm': state_mlstm_m[l]}
        xp, st_p = decoder_layer(xp, c_prompt, pos_p, l, lp, None)
        sp.append(st_p)
        xs, st_s = decoder_layer(xs, c_sample, pos_s, l, lp, past)
        ss.append(st_s)
    y_prompt = rms_norm(xp, final_g)
    y_sample = rms_norm(xs, final_g)

    def stk(states, name):
        return jnp.stack([s[name] for s in states], axis=0)

    return (y_prompt, y_sample,
            stk(sp, 'diff'), stk(sp, 'nsa'), stk(sp, 'win'), stk(sp, 'C'), stk(sp, 'n'), stk(sp, 'm'),
            stk(ss, 'diff'), stk(ss, 'nsa'), stk(ss, 'win'), stk(ss, 'C'), stk(ss, 'n'), stk(ss, 'm'))
```

```python
import functools
import math

import jax
import jax.numpy as jnp
import numpy as np
from jax import lax
from jax.experimental import pallas as pl
from jax.experimental.pallas import tpu as pltpu

D_MODEL = 1024
DEPTH = 4
PAST_LEN = 8192
EPS = 1e-6
ROPE_THETA = 10000.0
BRANCH_W = 512
N_BRANCH = 3
DA_HEADS = 4
DA_QK = 64
DA_V = 128
ML_HEADS = 4
ML_QK = 64
ML_V = 128
ML_CHUNK = 64
NSA_HEADS = 8
NSA_GROUPS = 2
NSA_HPG = NSA_HEADS // NSA_GROUPS
NSA_DH = 64
NSA_CMP_LEN = 32
NSA_CMP_STRIDE = 16
NSA_SEL_LEN = 64
NSA_TOPK = 16
NSA_WINDOW = 512
NSA_QBLOCK = 64
FORCE_SCORE = 1.0e4
N_EXPERTS = 32
TOP_K = 4
D_FF = 1024
SWIGLU_LIMIT = 7.0
SWIGLU_ALPHA = 1.702

LANES = 128
VMEM_LIMIT = 56 * 1024 * 1024
F32 = jnp.float32
BF16 = jnp.bfloat16

_C_QA, _C_KA, _C_VA, _C_QM, _C_KM, _C_VM, _C_OM, _C_IM, _C_FM, _C_QN, _C_KVN, _C_GN = (
    0, 512, 1024, 1536, 1792, 2048, 2560, 3072, 3076, 3080, 3592, 4360)

_IN_OUT_W = (1536, 1024, 512, 256, 1536, 128)
_W_IN_PACKED = 4992


def _in_plan():
    plan = []
    col = 0

    def add(width, rope, oi, oc):
        nonlocal col
        step = 256 if width % 256 == 0 else 128
        for s in range(0, width, step):
            plan.append((col + s, step, rope, oi, oc + s))
        col += width

    add(512, True, 0, 0)
    add(512, True, 0, 512)
    add(512, False, 0, 1024)
    add(512, True, 1, 0)
    add(512, False, 1, 512)
    add(128, False, 2, 0)
    add(128, False, 2, 128)
    add(128, True, 2, 256)
    add(128, False, 2, 384)
    add(128, True, 3, 0)
    add(128, False, 3, 128)
    add(256, False, 4, 0)
    add(256, False, 4, 256)
    add(512, False, 4, 512)
    add(512, False, 4, 1024)
    add(128, False, 5, 0)
    assert col == _W_IN_PACKED
    return tuple(plan)


_IN_PLAN = _in_plan()


def _pack_w_in(w_in, dtype):
    kvn = lambda j: w_in[:, _C_KVN + 128 * j:_C_KVN + 128 * (j + 1)]
    qn = w_in[:, _C_QN:_C_QN + 512]
    small = jnp.concatenate([w_in[:, _C_IM:_C_IM + 8], w_in[:, _C_GN:_C_GN + 24],
                             jnp.zeros((w_in.shape[0], 96), w_in.dtype)], axis=1)
    cols = [w_in[:, _C_QA:_C_QA + 512], qn, qn, w_in[:, _C_KA:_C_KA + 512], w_in[:, _C_VA:_C_VA + 512],
            kvn(0), kvn(1), kvn(2), kvn(3), kvn(4), kvn(5),
            w_in[:, _C_QM:_C_QM + 256], w_in[:, _C_KM:_C_KM + 256], w_in[:, _C_VM:_C_VM + 512],
            w_in[:, _C_OM:_C_OM + 512], small]
    return jnp.concatenate(cols, axis=1).astype(dtype)


def _rope_tables(pos):
    half = DA_QK // 2
    inv = ROPE_THETA ** (-jnp.arange(half, dtype=F32) / half)
    ang = pos.astype(F32)[:, None] * inv[None, :]
    cos, sin = jnp.cos(ang), jnp.sin(ang)
    cos_t = jnp.concatenate([cos, cos, cos, cos], axis=1)
    sin_t = jnp.concatenate([-sin, sin, -sin, sin], axis=1)
    return cos_t, sin_t


def _norm_mod(x, g, sc, sh):
    y = x * lax.rsqrt(jnp.mean(x * x, axis=-1, keepdims=True) + EPS)
    return (y * g) * (1.0 + sc) + sh


def _ada_body(c_ref, w_ref, b_ref, o_ref):
    c = c_ref[...]
    s = c * jax.nn.sigmoid(c)
    o_ref[0] = jnp.dot(s, w_ref[0], precision=lax.Precision.HIGHEST, preferred_element_type=F32) + b_ref[0]


def _ada_all(c_all, w_ada, b_ada):
    R = c_all.shape[0]
    tn = 512
    return pl.pallas_call(
        _ada_body,
        grid=(DEPTH, 6 * D_MODEL // tn),
        in_specs=[pl.BlockSpec((R, D_MODEL), lambda l, n: (0, 0)),
                  pl.BlockSpec((1, D_MODEL, tn), lambda l, n: (l, 0, n)),
                  pl.BlockSpec((1, 1, tn), lambda l, n: (l, 0, n))],
        out_specs=pl.BlockSpec((1, R, tn), lambda l, n: (l, 0, n)),
        out_shape=jax.ShapeDtypeStruct((DEPTH, R, 6 * D_MODEL), F32),
        compiler_params=pltpu.CompilerParams(vmem_limit_bytes=VMEM_LIMIT),
        name="ada",
    )(c_all, w_ada, b_ada.reshape(DEPTH, 1, 6 * D_MODEL))


def _rope128(y, cos, sin_signed, first_half):
    fwd = pltpu.roll(y, LANES - DA_QK // 2, 1)
    bwd = pltpu.roll(y, DA_QK // 2, 1)
    return y * cos + jnp.where(first_half, fwd, bwd) * sin_signed


def _in_body(x_ref, g_ref, sc_ref, sh_ref, cos_ref, sin_ref, w_ref, oq, odkv, onkv, owin, oml, osm, odkv_bf):
    outs = (oq, odkv, onkv, owin, oml, osm)
    h = _norm_mod(x_ref[...], g_ref[...], sc_ref[...], sh_ref[...])
    hb = h.astype(BF16)
    cos = cos_ref[...]
    sin = sin_ref[...]
    lane = lax.broadcasted_iota(jnp.int32, (1, LANES), 1)
    first_half = (lane % DA_QK) < (DA_QK // 2)
    for (c0, width, rope, oi, oc) in _IN_PLAN:
        y = jnp.dot(hb, w_ref[:, c0:c0 + width], preferred_element_type=F32)
        for s in range(0, width, LANES):
            ys = y[:, s:s + LANES]
            if rope:
                ys = _rope128(ys, cos, sin, first_half)
            outs[oi][:, oc + s:oc + s + LANES] = ys
            if oi == 1:
                odkv_bf[:, oc + s:oc + s + LANES] = ys.astype(BF16)


def _in_proj(x, g, sc, sh, cos_t, sin_t, w_packed, tm, rows_per_seq):
    T = x.shape[0]
    nt = T // tm
    tiles_per_seq = max(rows_per_seq // tm, 1)
    n_pos_tiles = cos_t.shape[0] // tm
    R = sc.shape[1]
    mod_spec = pl.BlockSpec((None, R, D_MODEL), lambda i: (i // tiles_per_seq, 0, 0))
    tab_spec = pl.BlockSpec((tm, LANES), lambda i: (i % n_pos_tiles, 0))
    widths = _IN_OUT_W
    out_shape = [jax.ShapeDtypeStruct((T, w), F32) for w in widths] + [jax.ShapeDtypeStruct((T, 1024), BF16)]
    out_specs = [pl.BlockSpec((tm, w), lambda i: (i, 0)) for w in widths] + [pl.BlockSpec((tm, 1024), lambda i: (i, 0))]
    return pl.pallas_call(
        _in_body,
        grid=(nt,),
        in_specs=[pl.BlockSpec((tm, D_MODEL), lambda i: (i, 0)),
                  pl.BlockSpec((1, D_MODEL), lambda i: (0, 0)),
                  mod_spec, mod_spec, tab_spec, tab_spec,
                  pl.BlockSpec((D_MODEL, _W_IN_PACKED), lambda i: (0, 0))],
        out_specs=out_specs,
        out_shape=out_shape,
        compiler_params=pltpu.CompilerParams(vmem_limit_bytes=VMEM_LIMIT),
        name="in_proj",
    )(x, g.reshape(1, D_MODEL), sc, sh, cos_t, sin_t, w_packed)


def _diff_body(q_ref, kv_ref, lam_ref, g_ref, o_ref, *, lam_init, tq):
    qi = pl.program_id(1)
    lv = lam_ref[...]
    lam = (jnp.exp(jnp.sum(lv[0:1] * lv[1:2], axis=1, keepdims=True))
           - jnp.exp(jnp.sum(lv[2:3] * lv[3:4], axis=1, keepdims=True)) + lam_init)
    lane = lax.broadcasted_iota(jnp.int32, (1, LANES), 1)
    row = lax.broadcasted_iota(jnp.int32, (2 * tq, 1), 0) % tq
    col = lax.broadcasted_iota(jnp.int32, (1, tq), 1)
    causal = col <= row
    for h in range(DA_HEADS):
        qh = q_ref[:, h * LANES:(h + 1) * LANES] * (DA_QK ** -0.5)
        qs = jnp.concatenate([jnp.where(lane < DA_QK, qh, 0.0), jnp.where(lane >= DA_QK, qh, 0.0)],
                             axis=0).astype(BF16)

        def step(kv, carry, masked):
            m, l, acc = carry
            start = pl.multiple_of(kv * tq, tq)
            k = kv_ref[pl.ds(start, tq), h * LANES:(h + 1) * LANES]
            v = kv_ref[pl.ds(start, tq), (DA_HEADS + h) * LANES:(DA_HEADS + h + 1) * LANES]
            s = lax.dot_general(qs, k, (((1,), (1,)), ((), ())), preferred_element_type=F32)
            if masked:
                s = jnp.where(causal, s, -1e30)
            m_new = jnp.maximum(m, jnp.max(s, axis=1, keepdims=True))
            alpha = jnp.exp(m - m_new)
            p = jnp.exp(s - m_new)
            l = alpha * l + jnp.sum(p, axis=1, keepdims=True)
            acc = alpha * acc + jnp.dot(p.astype(BF16), v, preferred_element_type=F32)
            return m_new, l, acc

        init = (jnp.full((2 * tq, 1), -1e30, F32), jnp.zeros((2 * tq, 1), F32), jnp.zeros((2 * tq, LANES), F32))
        carry = lax.fori_loop(0, qi, lambda kv, c: step(kv, c, False), init)
        m, l, acc = step(qi, carry, True)
        o = acc / l
        o = o[:tq] - lam * o[tq:]
        o = o * lax.rsqrt(jnp.mean(o * o, axis=-1, keepdims=True) + EPS) * g_ref[...]
        o_ref[:, h * LANES:(h + 1) * LANES] = o * (1.0 - lam_init)


def _diff_attn_prompt(oq, odkv_bf, da_lam, g_sub, lam_init, B, S, tq=256):
    nq = S // tq
    return pl.pallas_call(
        functools.partial(_diff_body, lam_init=lam_init, tq=tq),
        grid=(B, nq),
        in_specs=[pl.BlockSpec((tq, 512), lambda b, i: (b * nq + i, 0)),
                  pl.BlockSpec((S, 1024), lambda b, i: (b, 0)),
                  pl.BlockSpec((4, DA_QK), lambda b, i: (0, 0)),
                  pl.BlockSpec((1, DA_V), lambda b, i: (0, 0))],
        out_specs=pl.BlockSpec((tq, 512), lambda b, i: (b * nq + i, 0)),
        out_shape=jax.ShapeDtypeStruct((B * S, 512), F32),
        compiler_params=pltpu.CompilerParams(vmem_limit_bytes=VMEM_LIMIT),
        name="diff_attn",
    )(oq, odkv_bf, da_lam, g_sub.reshape(1, DA_V))


def _merge_body(x_ref, ya_ref, yb_ref, yc_ref, gm_ref, sc1_ref, sh1_ref, g1_ref, gf_ref, sc2_ref, sh2_ref,
                wbg_ref, bbg_ref, wbr_ref, wout_ref, wrh_ref, wrl_ref, br_ref, xo_ref, h2_ref, lg_ref):
    x = x_ref[...]
    hb = _norm_mod(x, gm_ref[...], sc1_ref[...], sh1_ref[...]).astype(BF16)
    mix = None
    for n, y_ref in enumerate((ya_ref, yb_ref, yc_ref)):
        gate = jax.nn.sigmoid(jnp.dot(hb, wbg_ref[:, n * D_MODEL:(n + 1) * D_MODEL], preferred_element_type=F32)
                              + bbg_ref[:, n * D_MODEL:(n + 1) * D_MODEL])
        proj = jnp.dot(y_ref[...].astype(BF16), wbr_ref[n], preferred_element_type=F32)
        mix = gate * proj if mix is None else mix + gate * proj
    y = jnp.dot(mix.astype(BF16), wout_ref[...], preferred_element_type=F32)
    xn = x + g1_ref[...] * y
    xo_ref[...] = xn
    h2 = _norm_mod(xn, gf_ref[...], sc2_ref[...], sh2_ref[...])
    hi = h2.astype(BF16)
    lo = (h2 - hi.astype(F32)).astype(BF16)
    h2_ref[...] = hi
    lg_ref[...] = (jnp.dot(hi, wrh_ref[...], preferred_element_type=F32)
                   + jnp.dot(lo, wrh_ref[...], preferred_element_type=F32)
                   + jnp.dot(hi, wrl_ref[...], preferred_element_type=F32) + br_ref[...])


def _merge(x, ya, yb, yc, gm, mods, gf, wbg, bbg, wbr, wout, wrh, wrl, br, tm, rows_per_seq):
    T = x.shape[0]
    nt = T // tm
    tiles_per_seq = max(rows_per_seq // tm, 1)
    sc1, sh1, g1, sc2, sh2 = mods
    R = sc1.shape[1]
    row = lambda w: pl.BlockSpec((tm, w), lambda i: (i, 0))
    mod_spec = pl.BlockSpec((None, R, D_MODEL), lambda i: (i // tiles_per_seq, 0, 0))
    const = lambda shape: pl.BlockSpec(shape, lambda i: (0,) * len(shape))
    return pl.pallas_call(
        _merge_body,
        grid=(nt,),
        in_specs=[row(D_MODEL), row(512), row(512), row(512), const((1, D_MODEL)), mod_spec, mod_spec, mod_spec,
                  const((1, D_MODEL)), mod_spec, mod_spec,
                  const((D_MODEL, 3 * D_MODEL)), const((1, 3 * D_MODEL)), const((3, BRANCH_W, D_MODEL)),
                  const((D_MODEL, D_MODEL)), const((D_MODEL, LANES)), const((D_MODEL, LANES)), const((1, LANES))],
        out_specs=[row(D_MODEL), row(D_MODEL), row(LANES)],
        out_shape=[jax.ShapeDtypeStruct((T, D_MODEL), F32), jax.ShapeDtypeStruct((T, D_MODEL), BF16),
                   jax.ShapeDtypeStruct((T, LANES), F32)],
        compiler_params=pltpu.CompilerParams(vmem_limit_bytes=VMEM_LIMIT),
        name="merge",
    )(x, ya, yb, yc, gm.reshape(1, D_MODEL), sc1, sh1, g1, gf.reshape(1, D_MODEL), sc2, sh2,
      wbg, bbg.reshape(1, 3 * D_MODEL), wbr, wout, wrh, wrl, br)


def _moe_body(te_ref, tv_ref, x_ref, wu_ref, bu_ref, wd_ref, bd_ref, o_ref, wu_s, wd_s):
    j = pl.program_id(0)
    e = te_ref[j]
    e_prev = te_ref[jnp.maximum(j - 1, 0)]

    @pl.when((j == 0) | (e != e_prev))
    def _():
        wu_s[...] = wu_ref[0].astype(BF16)
        wd_s[...] = wd_ref[0].astype(BF16)

    @pl.when(tv_ref[j] == 1)
    def _():
        uu = jnp.dot(x_ref[...], wu_s[...], preferred_element_type=F32) + bu_ref[0]
        gl = jnp.minimum(uu[:, :D_FF], SWIGLU_LIMIT)
        up = jnp.clip(uu[:, D_FF:], -SWIGLU_LIMIT, SWIGLU_LIMIT)
        act = gl * jax.nn.sigmoid(SWIGLU_ALPHA * gl) * (up + 1.0)
        o_ref[...] = jnp.dot(act.astype(BF16), wd_s[...], preferred_element_type=F32) + bd_ref[0]

    @pl.when(tv_ref[j] == 0)
    def _():
        o_ref[...] = jnp.zeros_like(o_ref)


def _moe_grouped(x_pad, tile_e, tile_valid, w_up, b_up, w_down, b_down, tm):
    n_tiles = x_pad.shape[0] // tm
    grid_spec = pltpu.PrefetchScalarGridSpec(
        num_scalar_prefetch=2,
        grid=(n_tiles,),
        in_specs=[pl.BlockSpec((tm, D_MODEL), lambda j, te, tv: (j, 0)),
                  pl.BlockSpec((1, D_MODEL, 2 * D_FF), lambda j, te, tv: (te[j], 0, 0)),
                  pl.BlockSpec((1, 1, 2 * D_FF), lambda j, te, tv: (te[j], 0, 0)),
                  pl.BlockSpec((1, D_FF, D_MODEL), lambda j, te, tv: (te[j], 0, 0)),
                  pl.BlockSpec((1, 1, D_MODEL), lambda j, te, tv: (te[j], 0, 0))],
        out_specs=pl.BlockSpec((tm, D_MODEL), lambda j, te, tv: (j, 0)),
        scratch_shapes=[pltpu.VMEM((D_MODEL, 2 * D_FF), BF16), pltpu.VMEM((D_FF, D_MODEL), BF16)],
    )
    return pl.pallas_call(
        _moe_body,
        grid_spec=grid_spec,
        out_shape=jax.ShapeDtypeStruct((x_pad.shape[0], D_MODEL), F32),
        compiler_params=pltpu.CompilerParams(vmem_limit_bytes=VMEM_LIMIT, dimension_semantics=("arbitrary",)),
        name="moe",
    )(tile_e, tile_valid, x_pad, w_up, b_up.reshape(N_EXPERTS, 1, 2 * D_FF), w_down,
      b_down.reshape(N_EXPERTS, 1, D_MODEL))


def _moe(h2, logits, w_up, b_up, w_down, b_down, tm=256):
    T = h2.shape[0]
    A = T * TOP_K
    top_v, top_i = lax.top_k(logits, TOP_K)
    wts = jax.nn.softmax(top_v, axis=-1)
    flat_e = top_i.reshape(A).astype(jnp.int32)
    order = jnp.argsort(flat_e, stable=True).astype(jnp.int32)
    sorted_e = flat_e[order]
    counts = jnp.zeros((N_EXPERTS,), jnp.int32).at[flat_e].add(1)
    tiles_per_e = (counts + tm - 1) // tm
    tile_end = jnp.cumsum(tiles_per_e)
    pad_start = (tile_end - tiles_per_e) * tm
    grp_start = jnp.cumsum(counts) - counts
    pos_sorted = pad_start[sorted_e] + (jnp.arange(A, dtype=jnp.int32) - grp_start[sorted_e])
    n_tiles = (A + N_EXPERTS * (tm - 1)) // tm + 1
    NP = n_tiles * tm
    src_tok = jnp.full((NP,), T, jnp.int32).at[pos_sorted].set(order // TOP_K)
    x_ext = jnp.concatenate([h2, jnp.zeros((1, D_MODEL), h2.dtype)], axis=0)
    x_pad = x_ext[src_tok]
    tile_idx = jnp.arange(n_tiles, dtype=jnp.int32)
    tile_e = jnp.minimum(jnp.searchsorted(tile_end, tile_idx, side='right'), N_EXPERTS - 1).astype(jnp.int32)
    tile_valid = (tile_idx < tile_end[-1]).astype(jnp.int32)
    last_e = tile_e[jnp.maximum(tile_end[-1] - 1, 0)]
    tile_e = jnp.where(tile_valid == 1, tile_e, last_e)
    y_pad = _moe_grouped(x_pad, tile_e, tile_valid, w_up, b_up, w_down, b_down, tm)
    inv_pos = jnp.zeros((A,), jnp.int32).at[order].set(pos_sorted)
    y_sel = y_pad[inv_pos].reshape(T, TOP_K, D_MODEL)
    return jnp.sum(wts[:, :, None] * y_sel, axis=1)


def _dot_hp(a, b):
    return jnp.dot(a, b, precision=lax.Precision.HIGHEST, preferred_element_type=F32)


_IN_ROPE_FLAGS = np.concatenate([np.full((w // LANES,), int(r), np.int32) for (_, w, r, _, _) in _IN_PLAN])


def _in_s_body(flag_ref, x_ref, g_ref, sc_ref, sh_ref, cos_ref, sin_ref, w_ref, o_ref):
    n = pl.program_id(0)
    h = _norm_mod(x_ref[...], g_ref[...], sc_ref[...], sh_ref[...])
    y = _dot_hp(h, w_ref[...])
    lane = lax.broadcasted_iota(jnp.int32, (1, LANES), 1)
    yr = _rope128(y, cos_ref[...], sin_ref[...], (lane % DA_QK) < (DA_QK // 2))
    o_ref[...] = jnp.where(flag_ref[n] == 1, yr, y)


def _in_proj_sample(x, g, sc, sh, cos_t, sin_t, w_packed):
    T = x.shape[0]
    full = lambda shape: pl.BlockSpec(shape, lambda n, f: (0,) * len(shape))
    grid_spec = pltpu.PrefetchScalarGridSpec(
        num_scalar_prefetch=1,
        grid=(_W_IN_PACKED // LANES,),
        in_specs=[full((T, D_MODEL)), full((1, D_MODEL)), full((T, D_MODEL)), full((T, D_MODEL)),
                  full((T, LANES)), full((T, LANES)),
                  pl.BlockSpec((D_MODEL, LANES), lambda n, f: (0, n))],
        out_specs=pl.BlockSpec((T, LANES), lambda n, f: (0, n)),
    )
    return pl.pallas_call(
        _in_s_body, grid_spec=grid_spec,
        out_shape=jax.ShapeDtypeStruct((T, _W_IN_PACKED), F32),
        name="in_proj_sample",
    )(jnp.asarray(_IN_ROPE_FLAGS), x, g.reshape(1, D_MODEL), sc, sh, cos_t, sin_t, w_packed)


def _mix_s_body(x_ref, g_ref, sc_ref, sh_ref, ya_ref, yb_ref, yc_ref, wg0, wg1, wg2, bg0, bg1, bg2, wbr_ref, o_ref):
    h = _norm_mod(x_ref[...], g_ref[...], sc_ref[...], sh_ref[...])
    mix = None
    for n, (y_ref, wg, bg) in enumerate(((ya_ref, wg0, bg0), (yb_ref, wg1, bg1), (yc_ref, wg2, bg2))):
        gate = jax.nn.sigmoid(_dot_hp(h, wg[...]) + bg[...])
        proj = _dot_hp(y_ref[...], wbr_ref[n])
        mix = gate * proj if mix is None else mix + gate * proj
    o_ref[...] = mix


def _gate_mix_sample(x, g, sc, sh, ya, yb, yc, w_bgate, b_bgate, w_branch, tn=256):
    T = x.shape[0]
    nj = D_MODEL // tn
    full = lambda shape: pl.BlockSpec(shape, lambda j: (0,) * len(shape))
    wg = [pl.BlockSpec((D_MODEL, tn), lambda j, n=n: (0, n * nj + j)) for n in range(N_BRANCH)]
    bg = [pl.BlockSpec((1, tn), lambda j, n=n: (0, n * nj + j)) for n in range(N_BRANCH)]
    bb = b_bgate.reshape(1, N_BRANCH * D_MODEL)
    return pl.pallas_call(
        _mix_s_body, grid=(nj,),
        in_specs=[full((T, D_MODEL)), full((1, D_MODEL)), full((T, D_MODEL)), full((T, D_MODEL)),
                  full((T, BRANCH_W)), full((T, BRANCH_W)), full((T, BRANCH_W))] + wg + bg
                 + [pl.BlockSpec((N_BRANCH, BRANCH_W, tn), lambda j: (0, 0, j))],
        out_specs=pl.BlockSpec((T, tn), lambda j: (0, j)),
        out_shape=jax.ShapeDtypeStruct((T, D_MODEL), F32),
        name="gate_mix_sample",
    )(x, g.reshape(1, D_MODEL), sc, sh, ya, yb, yc, w_bgate, w_bgate, w_bgate, bb, bb, bb, w_branch)


def _out_s_body(x_ref, mix_ref, g1_ref, wout_ref, gf_ref, sc2_ref, sh2_ref, wr_ref, br_ref, xo_ref, h2_ref, lg_ref):
    xn = x_ref[...] + g1_ref[...] * _dot_hp(mix_ref[...], wout_ref[...])
    xo_ref[...] = xn
    h2 = _norm_mod(xn, gf_ref[...], sc2_ref[...], sh2_ref[...])
    h2_ref[...] = h2
    lg_ref[...] = _dot_hp(h2, wr_ref[...]) + br_ref[...]


def _out_sample(x, mix, g1, w_out, gf, sc2, sh2, wr, br):
    T = x.shape[0]
    return pl.pallas_call(
        _out_s_body,
        out_shape=[jax.ShapeDtypeStruct((T, D_MODEL), F32), jax.ShapeDtypeStruct((T, D_MODEL), F32),
                   jax.ShapeDtypeStruct((T, LANES), F32)],
        compiler_params=pltpu.CompilerParams(vmem_limit_bytes=VMEM_LIMIT),
        name="out_sample",
    )(x, mix, g1, w_out, gf.reshape(1, D_MODEL), sc2, sh2, wr, br)


def _moe_s_body(h_ref, gate_ref, wu_ref, bu_ref, wd_ref, bd_ref, x_ref, g2_ref, o_ref, acc):
    e = pl.program_id(0)

    @pl.when(e == 0)
    def _():
        acc[...] = jnp.zeros_like(acc)

    uu = _dot_hp(h_ref[...], wu_ref[0]) + bu_ref[0]
    gl = jnp.minimum(uu[:, :D_FF], SWIGLU_LIMIT)
    up = jnp.clip(uu[:, D_FF:], -SWIGLU_LIMIT, SWIGLU_LIMIT)
    act = gl * jax.nn.sigmoid(SWIGLU_ALPHA * gl) * (up + 1.0)
    acc[...] += gate_ref[0] * (_dot_hp(act, wd_ref[0]) + bd_ref[0])

    @pl.when(e == N_EXPERTS - 1)
    def _():
        o_ref[...] = x_ref[...] + g2_ref[...] * acc[...]


def _moe_sample(h2, logits, w_up, b_up, w_down, b_down, x, g2):
    T = h2.shape[0]
    top_v, top_i = lax.top_k(logits, TOP_K)
    wts = jax.nn.softmax(top_v, axis=-1)
    gate = jnp.einsum('tk,tke->et', wts, jax.nn.one_hot(top_i, N_EXPERTS, dtype=F32),
                      precision=lax.Precision.HIGHEST)[:, :, None]
    full = lambda shape: pl.BlockSpec(shape, lambda e: (0,) * len(shape))
    return pl.pallas_call(
        _moe_s_body, grid=(N_EXPERTS,),
        in_specs=[full((T, D_MODEL)), pl.BlockSpec((1, T, 1), lambda e: (e, 0, 0)),
                  pl.BlockSpec((1, D_MODEL, 2 * D_FF), lambda e: (e, 0, 0)),
                  pl.BlockSpec((1, 1, 2 * D_FF), lambda e: (e, 0, 0)),
                  pl.BlockSpec((1, D_FF, D_MODEL), lambda e: (e, 0, 0)),
                  pl.BlockSpec((1, 1, D_MODEL), lambda e: (e, 0, 0)),
                  full((T, D_MODEL)), full((T, D_MODEL))],
        out_specs=full((T, D_MODEL)),
        out_shape=jax.ShapeDtypeStruct((T, D_MODEL), F32),
        scratch_shapes=[pltpu.VMEM((T, D_MODEL), F32)],
        compiler_params=pltpu.CompilerParams(vmem_limit_bytes=VMEM_LIMIT, dimension_semantics=("arbitrary",)),
        name="moe_sample",
    )(h2, gate, w_up, b_up.reshape(N_EXPERTS, 1, 2 * D_FF), w_down, b_down.reshape(N_EXPERTS, 1, D_MODEL), x, g2)


def _final_norm_body(x_ref, g_ref, o_ref):
    x = x_ref[...]
    o_ref[...] = x * lax.rsqrt(jnp.mean(x * x, axis=-1, keepdims=True) + EPS) * g_ref[...]


def _final_norm(x, g, tm):
    T = x.shape[0]
    return pl.pallas_call(
        _final_norm_body,
        grid=(T // tm,),
        in_specs=[pl.BlockSpec((tm, D_MODEL), lambda i: (i, 0)), pl.BlockSpec((1, D_MODEL), lambda i: (0, 0))],
        out_specs=pl.BlockSpec((tm, D_MODEL), lambda i: (i, 0)),
        out_shape=jax.ShapeDtypeStruct((T, D_MODEL), F32),
        name="final_norm",
    )(x, g.reshape(1, D_MODEL))


def _rms_norm(x, g):
    y = x * lax.rsqrt(jnp.mean(x * x, axis=-1, keepdims=True) + EPS)
    return y * g


def _masked_softmax(s, mask):
    s = jnp.where(mask, s, -1e30)
    m = jnp.max(s, axis=-1, keepdims=True)
    e = jnp.where(mask, jnp.exp(s - m), 0.0)
    return e / jnp.maximum(jnp.sum(e, axis=-1, keepdims=True), 1e-30)


def _sweep_queries(fn, qs, qpos, block):
    B, Q = qs[0].shape[:2]
    nb = Q // block

    def split(a):
        return jnp.moveaxis(a.reshape((B, nb, block) + a.shape[2:]), 1, 0)

    out = lax.map(lambda xs: fn(*xs[0], xs[1]), (tuple(split(a) for a in qs), qpos.reshape(nb, block)))
    out = jnp.moveaxis(out, 0, 1)
    return out.reshape((B, Q) + out.shape[3:])


def _diff_attn_core(q, k, v, qpos, kpos, lam, lam_init, g_sub):
    s = jnp.einsum('bqhcd,bkhcd->bhcqk', q, k) * (DA_QK ** -0.5)
    mask = kpos[None, :] <= qpos[:, None]
    p = _masked_softmax(s, mask)
    a = p[:, :, 0] - lam * p[:, :, 1]
    o = jnp.einsum('bhqk,bkhd->bqhd', a, v)
    return _rms_norm(o, g_sub) * (1.0 - lam_init)


def _mlstm_chunked(q, k, v, log_i, log_f, C0, n0, m0):
    B, S, H, DK = q.shape
    L = math.gcd(S, ML_CHUNK)
    nc = S // L

    def to_chunks(a):
        return jnp.moveaxis(a.reshape((B, nc, L) + a.shape[2:]), 1, 0)

    kf = k * (DK ** -0.5)
    causal = jnp.tril(jnp.ones((L, L), dtype=bool))[None, :, :, None]

    def step(carry, xs):
        C, n, m = carry
        qc, kc, vc, ic, fc = xs
        b = jnp.cumsum(fc, axis=1)
        a = b + m[:, None, :]
        D = jnp.where(causal, b[:, :, None, :] - b[:, None, :, :] + ic[:, None, :, :], -jnp.inf)
        mt = jnp.maximum(a, jnp.max(D, axis=2))
        w_inter = jnp.exp(a - mt)
        w_intra = jnp.exp(D - mt[:, :, None, :])
        sc = w_intra * jnp.einsum('bthd,bshd->btsh', qc, kc)
        num = w_inter[..., None] * jnp.einsum('bhvd,bthd->bthv', C, qc) + jnp.einsum('btsh,bshv->bthv', sc, vc)
        den = w_inter * jnp.einsum('bhd,bthd->bth', n, qc) + jnp.sum(sc, axis=2)
        h = num / jnp.maximum(jnp.abs(den), jnp.exp(-mt))[..., None]
        m_new = mt[:, -1]
        wi = w_intra[:, -1]
        wc = w_inter[:, -1]
        C_new = wc[..., None, None] * C + jnp.einsum('bsh,bshv,bshd->bhvd', wi, vc, kc)
        n_new = wc[..., None] * n + jnp.einsum('bsh,bshd->bhd', wi, kc)
        return (C_new, n_new, m_new), h

    (C1, n1, m1), hs = lax.scan(step, (C0, n0, m0),
                                (to_chunks(q), to_chunks(kf), to_chunks(v), to_chunks(log_i), to_chunks(log_f)))
    hs = jnp.moveaxis(hs, 0, 1).reshape(B, S, H, v.shape[-1])
    return hs, C1, n1, m1


def _nsa_compress(rows, pos_emb, w):
    B, T, G, dh = rows.shape
    Tp = -(-T // NSA_SEL_LEN) * NSA_SEL_LEN
    rows = jnp.pad(rows, ((0, 0), (0, Tp - T), (0, 0), (0, 0)))
    ch = rows.reshape(B, Tp // NSA_CMP_STRIDE, NSA_CMP_STRIDE, G, dh)
    n_sub = NSA_CMP_LEN // NSA_CMP_STRIDE
    nc = ch.shape[1] - n_sub + 1
    blocks = jnp.concatenate([ch[:, i:i + nc] for i in range(n_sub)], axis=2)
    blocks = blocks + pos_emb[None, None, :, None, :]
    return jnp.einsum('bnlgd,lde->bnge', blocks, w.reshape(NSA_CMP_LEN, dh, dh))


def _nsa_cmp_branch(q, kc, vc, qpos):
    B, Q, H, dh = q.shape
    nc = kc.shape[1]
    qg = q.reshape(B, Q, NSA_GROUPS, NSA_HPG, dh)
    s = jnp.einsum('bqgjd,bngd->bgjqn', qg, kc) * (dh ** -0.5)
    ends = jnp.arange(nc) * NSA_CMP_STRIDE + NSA_CMP_LEN - 1
    mask = ends[None, :] <= qpos[:, None]
    p = _masked_softmax(s, mask)
    o = jnp.einsum('bgjqn,bngd->bqgjd', p, vc).reshape(B, Q, H, dh)
    return o, jnp.sum(p, axis=2)


def _nsa_select(imp_cmp, qpos, nsb):
    r = NSA_SEL_LEN // NSA_CMP_STRIDE
    imp = jnp.pad(imp_cmp, ((0, 0), (0, 0), (0, 0), (0, nsb * r - imp_cmp.shape[-1])))
    imp = jnp.sum(imp.reshape(imp.shape[:3] + (nsb, r)), axis=-1)
    j = jnp.arange(nsb)[None, :]
    qb = (qpos // NSA_SEL_LEN)[:, None]
    forced = (j == 0) | (j == qb) | (j == qb - 1)
    score = jnp.where(forced, FORCE_SCORE, imp)
    score = jnp.where(j <= qb, score, -FORCE_SCORE)
    _, idx = lax.top_k(score, min(NSA_TOPK, nsb))
    valid = idx <= (qpos // NSA_SEL_LEN)[None, None, :, None]
    return jnp.transpose(idx, (0, 2, 1, 3)), jnp.transpose(valid, (0, 2, 1, 3))


def _to_sel_blocks(rows, nsb):
    B, T, G, dh = rows.shape
    rows = jnp.pad(rows, ((0, 0), (0, nsb * NSA_SEL_LEN - T), (0, 0), (0, 0)))
    return jnp.transpose(rows.reshape(B, nsb, NSA_SEL_LEN, G, dh), (0, 3, 1, 2, 4))


def _nsa_slc_branch(q, idx, valid, qpos, kblk, vblk):
    B, Qb, H, dh = q.shape
    bi = jnp.arange(B)[:, None, None, None]
    gi = jnp.arange(NSA_GROUPS)[None, None, :, None]
    ks = kblk[bi, gi, idx]
    vs = vblk[bi, gi, idx]
    kpos = idx[..., None] * NSA_SEL_LEN + jnp.arange(NSA_SEL_LEN)
    mask = valid[..., None] & (kpos <= qpos[None, :, None, None, None])
    qg = q.reshape(B, Qb, NSA_GROUPS, NSA_HPG, dh)
    s = jnp.einsum('bqgjd,bqgnld->bqgjnl', qg, ks) * (dh ** -0.5)
    p = _masked_softmax(s.reshape(B, Qb, NSA_GROUPS, NSA_HPG, -1), mask.reshape(B, Qb, NSA_GROUPS, 1, -1))
    o = jnp.einsum('bqgjm,bqgmd->bqgjd', p, vs.reshape(B, Qb, NSA_GROUPS, -1, dh))
    return o.reshape(B, Qb, H, dh)


def _nsa_win_branch(q, kw, vw, qpos, kpos):
    B, NB, QB, H, dh = q.shape
    qg = q.reshape(B, NB, QB, NSA_GROUPS, NSA_HPG, dh)
    s = jnp.einsum('bnqgjd,bnkgd->bngjqk', qg, kw) * (dh ** -0.5)
    dpos = qpos[:, :, None] - kpos[:, None, :]
    mask = (dpos >= 0) & (dpos < NSA_WINDOW) & (kpos[:, None, :] >= 0)
    p = _masked_softmax(s, mask[None, :, None, None])
    o = jnp.einsum('bngjqk,bnkgd->bnqgjd', p, vw)
    return o.reshape(B, NB * QB, H, dh)


def _gather_pages(cache, l, page_table):
    g = cache[l, page_table]
    return g.reshape((g.shape[0], g.shape[1] * g.shape[2]) + g.shape[3:])


def _mlstm_mixer(oml, osm, gate_b, norm_g, B, Q, past):
    qm = oml[:, 0:256].reshape(B, Q, ML_HEADS, ML_QK)
    km = oml[:, 256:512].reshape(B, Q, ML_HEADS, ML_QK)
    vm = oml[:, 512:1024].reshape(B, Q, ML_HEADS, ML_V)
    om = oml[:, 1024:1536].reshape(B, Q, ML_HEADS, ML_V)
    im = osm[:, 0:4].reshape(B, Q, ML_HEADS)
    fm = osm[:, 4:8].reshape(B, Q, ML_HEADS)
    log_i = im + gate_b[:ML_HEADS]
    log_f = jax.nn.log_sigmoid(fm + gate_b[ML_HEADS:])
    if past is None:
        C0 = jnp.zeros((B, ML_HEADS, ML_V, ML_QK), F32)
        n0 = jnp.zeros((B, ML_HEADS, ML_QK), F32)
        m0 = jnp.zeros((B, ML_HEADS), F32)
    else:
        C0, n0, m0 = past
    hm, C1, n1, m1 = _mlstm_chunked(qm, km, vm, log_i, log_f, C0, n0, m0)
    hm = _rms_norm(hm, norm_g) * jax.nn.sigmoid(om)
    return hm.reshape(B * Q, ML_HEADS * ML_V), C1, n1, m1


def _nsa_mixer(oq, onkv, owin, osm, pos, cmp_pos, cmp_w, B, Q, past):
    qn_rot = oq[:, 512:1024].reshape(B, Q, NSA_HEADS, NSA_DH)
    qn = oq[:, 1024:1536].reshape(B, Q, NSA_HEADS, NSA_DH)
    nk = onkv.reshape(B, Q, 4, NSA_GROUPS, NSA_DH)
    cmp_k, cmp_v, slc_k, slc_v = nk[:, :, 0], nk[:, :, 1], nk[:, :, 2], nk[:, :, 3]
    wk = owin.reshape(B, Q, 2, NSA_GROUPS, NSA_DH)
    win_k, win_v = wk[:, :, 0], wk[:, :, 1]
    if past is not None:
        pn, buf = past
        cmp_k = jnp.concatenate([pn[:, :, 0], cmp_k], axis=1)
        cmp_v = jnp.concatenate([pn[:, :, 1], cmp_v], axis=1)
        slc_k = jnp.concatenate([pn[:, :, 2], slc_k], axis=1)
        slc_v = jnp.concatenate([pn[:, :, 3], slc_v], axis=1)
    nsb = -(-cmp_k.shape[1] // NSA_SEL_LEN)
    kc = _nsa_compress(cmp_k, cmp_pos[0], cmp_w[0])
    vc = _nsa_compress(cmp_v, cmp_pos[1], cmp_w[1])
    o_cmp, imp = _nsa_cmp_branch(qn, kc, vc, pos)
    idx, valid = _nsa_select(imp, pos, nsb)
    kblk = _to_sel_blocks(slc_k, nsb)
    vblk = _to_sel_blocks(slc_v, nsb)
    o_slc = _sweep_queries(lambda q, ix, ok, p: _nsa_slc_branch(q, ix, ok, p, kblk, vblk),
                           (qn_rot, idx, valid), pos, math.gcd(Q, NSA_QBLOCK))
    if past is None:
        win_len = min(NSA_WINDOW, Q)
        QB = math.gcd(Q, NSA_QBLOCK)
        nb, nw = Q // QB, NSA_WINDOW // QB
        kp = jnp.pad(win_k, ((0, 0), (NSA_WINDOW, 0), (0, 0), (0, 0))).reshape(B, nb + nw, QB, NSA_GROUPS, NSA_DH)
        vp = jnp.pad(win_v, ((0, 0), (NSA_WINDOW, 0), (0, 0), (0, 0))).reshape(B, nb + nw, QB, NSA_GROUPS, NSA_DH)
        band_k = jnp.concatenate([kp[:, i:i + nb] for i in range(nw + 1)], axis=2)
        band_v = jnp.concatenate([vp[:, i:i + nb] for i in range(nw + 1)], axis=2)
        kpos_w = (jnp.arange(nb) * QB)[:, None] - NSA_WINDOW + jnp.arange((nw + 1) * QB)[None, :]
        o_win = _nsa_win_branch(qn_rot.reshape(B, nb, QB, NSA_HEADS, NSA_DH), band_k, band_v,
                                pos.reshape(nb, QB), kpos_w)
        new_win = jnp.stack([win_k, win_v], axis=2)[:, Q - win_len:]
    else:
        win_len = buf.shape[1]
        kw = jnp.concatenate([buf[:, :, 0], win_k], axis=1)
        vw = jnp.concatenate([buf[:, :, 1], win_v], axis=1)
        kpos_w = (pos[0] - win_len) + jnp.arange(win_len + Q)
        o_win = _nsa_win_branch(qn_rot[:, None], kw[:, None], vw[:, None], pos[None, :], kpos_w[None, :])
        new_win = jnp.stack([kw, vw], axis=2)[:, -win_len:]
    gn = jax.nn.sigmoid(osm[:, 8:32].reshape(B, Q, NSA_HEADS, 3))
    yc = (gn[..., 0:1] * o_cmp + gn[..., 1:2] * o_slc + gn[..., 2:3] * o_win).reshape(B * Q, NSA_HEADS * NSA_DH)
    return yc, new_win


def _diff_attn_sample(oq, odkv, pd, pos, da_lam, g_sub, lam_init, B, Q):
    qa = oq[:, 0:512].reshape(B, Q, DA_HEADS, 2, DA_QK)
    ka = odkv[:, 0:512].reshape(B, Q, DA_HEADS, 2, DA_QK)
    va = odkv[:, 512:1024].reshape(B, Q, DA_HEADS, DA_V)
    k_all = jnp.concatenate([pd[:, :, 0].reshape(B, -1, DA_HEADS, 2, DA_QK), ka], axis=1)
    v_all = jnp.concatenate([pd[:, :, 1], va], axis=1)
    kpos = jnp.arange(k_all.shape[1])
    lam = jnp.exp(jnp.sum(da_lam[0] * da_lam[1])) - jnp.exp(jnp.sum(da_lam[2] * da_lam[3])) + lam_init
    ya = _diff_attn_core(qa, k_all, v_all, pos, kpos, lam, lam_init, g_sub)
    return ya.reshape(B * Q, DA_HEADS * DA_V)


def kernel(x_prompt, x_sample, cache_diff_kv, cache_nsa_kv, state_nsa_win, state_mlstm_C, state_mlstm_n, state_mlstm_m, page_table, c_prompt, c_sample, norm_mix_g, norm_ffn_g, w_ada, b_ada, w_in, da_lam, da_subln_g, ml_gate_b, ml_norm_g, nsa_cmp_pos, nsa_cmp_w, w_branch, w_bgate, b_bgate, w_out, w_router, b_router, w_up, b_up, w_down, b_down, final_g):
    B, S, D = x_prompt.shape
    Bs, Qs, _ = x_sample.shape
    Tp, Ts = B * S, Bs * Qs
    tm_p = 256
    pos_p = jnp.arange(S)
    pos_s = PAST_LEN + jnp.arange(Qs)
    cos_p, sin_p = _rope_tables(pos_p)
    cos_s, sin_s = _rope_tables(jnp.broadcast_to(pos_s[None, :], (Bs, Qs)).reshape(Ts))

    mods = _ada_all(jnp.concatenate([c_prompt, c_sample], axis=0), w_ada, b_ada)

    xp = x_prompt.reshape(Tp, D)
    xs = x_sample.reshape(Ts, D)
    sp = {k: [] for k in ('diff', 'nsa', 'win', 'C', 'n', 'm')}
    ss = {k: [] for k in ('diff', 'nsa', 'win', 'C', 'n', 'm')}
    for l in range(DEPTH):
        lam_init = 0.8 - 0.6 * math.exp(-0.3 * l)
        w_packed = _pack_w_in(w_in[l], BF16)
        wbg = w_bgate[l].astype(BF16)
        wbr = w_branch[l].astype(BF16)
        wout = w_out[l].astype(BF16)
        wr = jnp.pad(w_router[l], ((0, 0), (0, LANES - N_EXPERTS)))
        wrh = wr.astype(BF16)
        wrl = (wr - wrh.astype(F32)).astype(BF16)
        br = jnp.pad(b_router[l], (0, LANES - N_EXPERTS)).reshape(1, LANES)
        mod_p = [m.reshape(B, 1, D) for m in jnp.split(mods[l, :B], 6, axis=-1)]
        mod_s = [m.reshape(1, Ts, D) for m in jnp.split(mods[l, B:], 6, axis=-1)]

        oq, odkv, onkv, owin, oml, osm, odkv_bf = _in_proj(xp, norm_mix_g[l], mod_p[1], mod_p[0], cos_p, sin_p,
                                                           w_packed, tm_p, S)
        ya = _diff_attn_prompt(oq, odkv_bf, da_lam[l], da_subln_g[l], lam_init, B, S)
        yb, C1, n1, m1 = _mlstm_mixer(oml, osm, ml_gate_b[l], ml_norm_g[l], B, S, None)
        yc, new_win = _nsa_mixer(oq, onkv, owin, osm, pos_p, nsa_cmp_pos[l], nsa_cmp_w[l], B, S, None)
        xp, h2p, lgp = _merge(xp, ya, yb, yc, norm_mix_g[l], (mod_p[1], mod_p[0], mod_p[2], mod_p[4], mod_p[3]),
                              norm_ffn_g[l], wbg, b_bgate[l], wbr, wout, wrh, wrl, br, tm_p, S)
        sp['diff'].append(odkv.reshape(B, S, 2, DA_HEADS, DA_V))
        sp['nsa'].append(onkv.reshape(B, S, 4, NSA_GROUPS, NSA_DH))
        sp['win'].append(new_win)
        sp['C'].append(C1); sp['n'].append(n1); sp['m'].append(m1)

        sc1s, sh1s = mod_s[1].reshape(Ts, D), mod_s[0].reshape(Ts, D)
        u = _in_proj_sample(xs, norm_mix_g[l], sc1s, sh1s, cos_s, sin_s, _pack_w_in(w_in[l], F32))
        oq, odkv, onkv, owin, oml, osm = jnp.split(u, [int(c) for c in np.cumsum(_IN_OUT_W)[:-1]], axis=1)
        with jax.default_matmul_precision("highest"):
            pd = _gather_pages(cache_diff_kv, l, page_table)
            pn = _gather_pages(cache_nsa_kv, l, page_table)
            ya = _diff_attn_sample(oq, odkv, pd, pos_s, da_lam[l], da_subln_g[l], lam_init, Bs, Qs)
            yb, C1, n1, m1 = _mlstm_mixer(oml, osm, ml_gate_b[l], ml_norm_g[l], Bs, Qs,
                                          (state_mlstm_C[l], state_mlstm_n[l], state_mlstm_m[l]))
            yc, new_win = _nsa_mixer(oq, onkv, owin, osm, pos_s, nsa_cmp_pos[l], nsa_cmp_w[l], Bs, Qs,
                                     (pn, state_nsa_win[l]))
        mix = _gate_mix_sample(xs, norm_mix_g[l], sc1s, sh1s, ya, yb, yc, w_bgate[l], b_bgate[l], w_branch[l])
        xs, h2s, lgs = _out_sample(xs, mix, mod_s[2].reshape(Ts, D), w_out[l], norm_ffn_g[l],
                                   mod_s[4].reshape(Ts, D), mod_s[3].reshape(Ts, D), wr, br)
        xs = _moe_sample(h2s, lgs[:, :N_EXPERTS], w_up[l], b_up[l], w_down[l], b_down[l], xs,
                         mod_s[5].reshape(Ts, D))
        ss['diff'].append(odkv.reshape(Bs, Qs, 2, DA_HEADS, DA_V))
        ss['nsa'].append(onkv.reshape(Bs, Qs, 4, NSA_GROUPS, NSA_DH))
        ss['win'].append(new_win)
        ss['C'].append(C1); ss['n'].append(n1); ss['m'].append(m1)

        ym = _moe(h2p, lgp[:, :N_EXPERTS], w_up[l], b_up[l], w_down[l], b_down[l])
        xp = xp + jnp.repeat(mod_p[5].reshape(B, D), S, axis=0) * ym

    y_prompt = _final_norm(xp, final_g, tm_p).reshape(B, S, D)
    y_sample = _final_norm(xs, final_g, Ts).reshape(Bs, Qs, D)
    stk = lambda d, k: jnp.stack(d[k], axis=0)
    return (y_prompt, y_sample,
            stk(sp, 'diff'), stk(sp, 'nsa'), stk(sp, 'win'), stk(sp, 'C'), stk(sp, 'n'), stk(sp, 'm'),
            stk(ss, 'diff'), stk(ss, 'nsa'), stk(ss, 'win'), stk(ss, 'C'), stk(ss, 'n'), stk(ss, 'm'))
```

```python
import functools
import math

import jax
import jax.numpy as jnp
import numpy as np
from jax import lax
from jax.experimental import pallas as pl
from jax.experimental.pallas import tpu as pltpu

D_MODEL = 1024
DEPTH = 4
PAST_LEN = 8192
EPS = 1e-6
ROPE_THETA = 10000.0
BRANCH_W = 512
N_BRANCH = 3
DA_HEADS = 4
DA_QK = 64
DA_V = 128
ML_HEADS = 4
ML_QK = 64
ML_V = 128
ML_CHUNK = 64
NSA_HEADS = 8
NSA_GROUPS = 2
NSA_HPG = NSA_HEADS // NSA_GROUPS
NSA_DH = 64
NSA_CMP_LEN = 32
NSA_CMP_STRIDE = 16
NSA_SEL_LEN = 64
NSA_TOPK = 16
NSA_WINDOW = 512
NSA_QBLOCK = 64
FORCE_SCORE = 1.0e4
N_EXPERTS = 32
TOP_K = 4
D_FF = 1024
SWIGLU_LIMIT = 7.0
SWIGLU_ALPHA = 1.702

LANES = 128
VMEM_LIMIT = 56 * 1024 * 1024
F32 = jnp.float32
BF16 = jnp.bfloat16

_C_QA, _C_KA, _C_VA, _C_QM, _C_KM, _C_VM, _C_OM, _C_IM, _C_FM, _C_QN, _C_KVN, _C_GN = (
    0, 512, 1024, 1536, 1792, 2048, 2560, 3072, 3076, 3080, 3592, 4360)

_IN_OUT_W = (1536, 1024, 512, 256, 1536, 128)
_W_IN_PACKED = 4992


def _in_plan():
    plan = []
    col = 0

    def add(width, rope, oi, oc):
        nonlocal col
        step = 256 if width % 256 == 0 else 128
        for s in range(0, width, step):
            plan.append((col + s, step, rope, oi, oc + s))
        col += width

    add(512, True, 0, 0)
    add(512, True, 0, 512)
    add(512, False, 0, 1024)
    add(512, True, 1, 0)
    add(512, False, 1, 512)
    add(128, False, 2, 0)
    add(128, False, 2, 128)
    add(128, True, 2, 256)
    add(128, False, 2, 384)
    add(128, True, 3, 0)
    add(128, False, 3, 128)
    add(256, False, 4, 0)
    add(256, False, 4, 256)
    add(512, False, 4, 512)
    add(512, False, 4, 1024)
    add(128, False, 5, 0)
    assert col == _W_IN_PACKED
    return tuple(plan)


_IN_PLAN = _in_plan()

_NSA_BF_W = 640
_W_IN_EXT = _W_IN_PACKED + NSA_GROUPS * _NSA_BF_W
_IN_PLAN_EXT = tuple(
    (_W_IN_PACKED + g * _NSA_BF_W + c, 128 if c == 512 else 256, c < 512, 6, g * _NSA_BF_W + c)
    for g in range(NSA_GROUPS) for c in (0, 256, 512))


def _pack_w_in_ext(w_in):
    kvn = lambda j, g: w_in[:, _C_KVN + 128 * j + 64 * g:_C_KVN + 128 * j + 64 * (g + 1)].astype(BF16)
    cols = [_pack_w_in(w_in, BF16)]
    for g in range(NSA_GROUPS):
        cols += [kvn(2, g)] * NSA_HPG + [kvn(4, g)] * NSA_HPG + [kvn(3, g), kvn(5, g)]
    return jnp.concatenate(cols, axis=1)


def _pack_w_in(w_in, dtype):
    kvn = lambda j: w_in[:, _C_KVN + 128 * j:_C_KVN + 128 * (j + 1)]
    qn = w_in[:, _C_QN:_C_QN + 512]
    small = jnp.concatenate([w_in[:, _C_IM:_C_IM + 8], w_in[:, _C_GN:_C_GN + 24],
                             jnp.zeros((w_in.shape[0], 96), w_in.dtype)], axis=1)
    cols = [w_in[:, _C_QA:_C_QA + 512], qn, qn, w_in[:, _C_KA:_C_KA + 512], w_in[:, _C_VA:_C_VA + 512],
            kvn(0), kvn(1), kvn(2), kvn(3), kvn(4), kvn(5),
            w_in[:, _C_QM:_C_QM + 256], w_in[:, _C_KM:_C_KM + 256], w_in[:, _C_VM:_C_VM + 512],
            w_in[:, _C_OM:_C_OM + 512], small]
    return jnp.concatenate(cols, axis=1).astype(dtype)


def _rope_tables(pos):
    half = DA_QK // 2
    inv = ROPE_THETA ** (-jnp.arange(half, dtype=F32) / half)
    ang = pos.astype(F32)[:, None] * inv[None, :]
    cos, sin = jnp.cos(ang), jnp.sin(ang)
    cos_t = jnp.concatenate([cos, cos, cos, cos], axis=1)
    sin_t = jnp.concatenate([-sin, sin, -sin, sin], axis=1)
    return cos_t, sin_t


def _norm_mod(x, g, sc, sh):
    y = x * lax.rsqrt(jnp.mean(x * x, axis=-1, keepdims=True) + EPS)
    return (y * g) * (1.0 + sc) + sh


def _ada_body(c_ref, w_ref, b_ref, o_ref):
    c = c_ref[...]
    s = c * jax.nn.sigmoid(c)
    o_ref[0] = jnp.dot(s, w_ref[0], precision=lax.Precision.HIGHEST, preferred_element_type=F32) + b_ref[0]


def _ada_all(c_all, w_ada, b_ada):
    R = c_all.shape[0]
    tn = 512
    return pl.pallas_call(
        _ada_body,
        grid=(DEPTH, 6 * D_MODEL // tn),
        in_specs=[pl.BlockSpec((R, D_MODEL), lambda l, n: (0, 0)),
                  pl.BlockSpec((1, D_MODEL, tn), lambda l, n: (l, 0, n)),
                  pl.BlockSpec((1, 1, tn), lambda l, n: (l, 0, n))],
        out_specs=pl.BlockSpec((1, R, tn), lambda l, n: (l, 0, n)),
        out_shape=jax.ShapeDtypeStruct((DEPTH, R, 6 * D_MODEL), F32),
        compiler_params=pltpu.CompilerParams(vmem_limit_bytes=VMEM_LIMIT),
        name="ada",
    )(c_all, w_ada, b_ada.reshape(DEPTH, 1, 6 * D_MODEL))


def _rope128(y, cos, sin_signed, first_half):
    fwd = pltpu.roll(y, LANES - DA_QK // 2, 1)
    bwd = pltpu.roll(y, DA_QK // 2, 1)
    return y * cos + jnp.where(first_half, fwd, bwd) * sin_signed


def _in_body(x_ref, g_ref, sc_ref, sh_ref, cos_ref, sin_ref, w_ref, oq, odkv, onkv, owin, oml, osm, odkv_bf, onsa_bf):
    outs = (oq, odkv, onkv, owin, oml, osm, onsa_bf)
    h = _norm_mod(x_ref[...], g_ref[...], sc_ref[...], sh_ref[...])
    hb = h.astype(BF16)
    cos = cos_ref[...]
    sin = sin_ref[...]
    lane = lax.broadcasted_iota(jnp.int32, (1, LANES), 1)
    first_half = (lane % DA_QK) < (DA_QK // 2)
    for (c0, width, rope, oi, oc) in _IN_PLAN + _IN_PLAN_EXT:
        y = jnp.dot(hb, w_ref[:, c0:c0 + width], preferred_element_type=F32)
        for s in range(0, width, LANES):
            ys = y[:, s:s + LANES]
            if rope:
                ys = _rope128(ys, cos, sin, first_half)
            outs[oi][:, oc + s:oc + s + LANES] = ys.astype(outs[oi].dtype)
            if oi == 1:
                odkv_bf[:, oc + s:oc + s + LANES] = ys.astype(BF16)


def _in_proj(x, g, sc, sh, cos_t, sin_t, w_packed, tm, rows_per_seq):
    T = x.shape[0]
    nt = T // tm
    tiles_per_seq = max(rows_per_seq // tm, 1)
    n_pos_tiles = cos_t.shape[0] // tm
    R = sc.shape[1]
    mod_spec = pl.BlockSpec((None, R, D_MODEL), lambda i: (i // tiles_per_seq, 0, 0))
    tab_spec = pl.BlockSpec((tm, LANES), lambda i: (i % n_pos_tiles, 0))
    widths = _IN_OUT_W
    bf_widths = (1024, NSA_GROUPS * _NSA_BF_W)
    out_shape = ([jax.ShapeDtypeStruct((T, w), F32) for w in widths]
                 + [jax.ShapeDtypeStruct((T, w), BF16) for w in bf_widths])
    out_specs = [pl.BlockSpec((tm, w), lambda i: (i, 0)) for w in widths + bf_widths]
    return pl.pallas_call(
        _in_body,
        grid=(nt,),
        in_specs=[pl.BlockSpec((tm, D_MODEL), lambda i: (i, 0)),
                  pl.BlockSpec((1, D_MODEL), lambda i: (0, 0)),
                  mod_spec, mod_spec, tab_spec, tab_spec,
                  pl.BlockSpec((D_MODEL, _W_IN_EXT), lambda i: (0, 0))],
        out_specs=out_specs,
        out_shape=out_shape,
        compiler_params=pltpu.CompilerParams(vmem_limit_bytes=VMEM_LIMIT),
        name="in_proj",
    )(x, g.reshape(1, D_MODEL), sc, sh, cos_t, sin_t, w_packed)


def _diff_body(q_ref, kv_ref, lam_ref, g_ref, o_ref, *, lam_init, tq):
    qi = pl.program_id(1)
    lv = lam_ref[...]
    lam = (jnp.exp(jnp.sum(lv[0:1] * lv[1:2], axis=1, keepdims=True))
           - jnp.exp(jnp.sum(lv[2:3] * lv[3:4], axis=1, keepdims=True)) + lam_init)
    lane = lax.broadcasted_iota(jnp.int32, (1, LANES), 1)
    row = lax.broadcasted_iota(jnp.int32, (2 * tq, 1), 0) % tq
    col = lax.broadcasted_iota(jnp.int32, (1, tq), 1)
    causal = col <= row
    for h in range(DA_HEADS):
        qh = q_ref[:, h * LANES:(h + 1) * LANES] * (DA_QK ** -0.5)
        qs = jnp.concatenate([jnp.where(lane < DA_QK, qh, 0.0), jnp.where(lane >= DA_QK, qh, 0.0)],
                             axis=0).astype(BF16)

        def step(kv, carry, masked):
            m, l, acc = carry
            start = pl.multiple_of(kv * tq, tq)
            k = kv_ref[pl.ds(start, tq), h * LANES:(h + 1) * LANES]
            v = kv_ref[pl.ds(start, tq), (DA_HEADS + h) * LANES:(DA_HEADS + h + 1) * LANES]
            s = lax.dot_general(qs, k, (((1,), (1,)), ((), ())), preferred_element_type=F32)
            if masked:
                s = jnp.where(causal, s, -1e30)
            m_new = jnp.maximum(m, jnp.max(s, axis=1, keepdims=True))
            alpha = jnp.exp(m - m_new)
            p = jnp.exp(s - m_new)
            l = alpha * l + jnp.sum(p, axis=1, keepdims=True)
            acc = alpha * acc + jnp.dot(p.astype(BF16), v, preferred_element_type=F32)
            return m_new, l, acc

        init = (jnp.full((2 * tq, 1), -1e30, F32), jnp.zeros((2 * tq, 1), F32), jnp.zeros((2 * tq, LANES), F32))
        carry = lax.fori_loop(0, qi, lambda kv, c: step(kv, c, False), init)
        m, l, acc = step(qi, carry, True)
        o = acc / l
        o = o[:tq] - lam * o[tq:]
        o = o * lax.rsqrt(jnp.mean(o * o, axis=-1, keepdims=True) + EPS) * g_ref[...]
        o_ref[:, h * LANES:(h + 1) * LANES] = o * (1.0 - lam_init)


def _diff_attn_prompt(oq, odkv_bf, da_lam, g_sub, lam_init, B, S, tq=256):
    nq = S // tq
    return pl.pallas_call(
        functools.partial(_diff_body, lam_init=lam_init, tq=tq),
        grid=(B, nq),
        in_specs=[pl.BlockSpec((tq, 512), lambda b, i: (b * nq + i, 0)),
                  pl.BlockSpec((S, 1024), lambda b, i: (b, 0)),
                  pl.BlockSpec((4, DA_QK), lambda b, i: (0, 0)),
                  pl.BlockSpec((1, DA_V), lambda b, i: (0, 0))],
        out_specs=pl.BlockSpec((tq, 512), lambda b, i: (b * nq + i, 0)),
        out_shape=jax.ShapeDtypeStruct((B * S, 512), F32),
        compiler_params=pltpu.CompilerParams(vmem_limit_bytes=VMEM_LIMIT),
        name="diff_attn",
    )(oq, odkv_bf, da_lam, g_sub.reshape(1, DA_V))


_NSA_NC = 256
_NSA_R = NSA_SEL_LEN // NSA_CMP_STRIDE


def _cmp_body(xc_ref, w_ref, pos_ref, wfull_ref, o_ref):
    row = lax.broadcasted_iota(jnp.int32, (_NSA_NC, 1), 0)
    lane = lax.broadcasted_iota(jnp.int32, (1, LANES), 1)
    for kind in range(2):
        const = jnp.dot(pos_ref[kind], wfull_ref[kind], precision=lax.Precision.HIGHEST,
                        preferred_element_type=F32)
        const = jnp.concatenate([const, jnp.zeros_like(const)], axis=1)
        for g in range(NSA_GROUPS):
            pq = jnp.dot(xc_ref[2 * kind + g], w_ref[kind], preferred_element_type=F32)
            nxt = pltpu.roll(pltpu.roll(pq, _NSA_NC - 1, 0), NSA_DH, 1)
            y = pq + nxt + const
            o_ref[2 * kind + g] = jnp.where((row < _NSA_NC - 1) & (lane < NSA_DH), y, 0.0)


def _nsa_compress_prompt(onkv, cmp_pos, cmp_w, B, S):
    n_ch = S // NSA_CMP_STRIDE
    xc = onkv[:, 0:256].reshape(B, n_ch, NSA_CMP_STRIDE, 4, NSA_DH)
    xc = jnp.transpose(xc, (0, 3, 1, 2, 4)).reshape(B, 4, n_ch, NSA_CMP_STRIDE * NSA_DH).astype(BF16)
    half = NSA_CMP_STRIDE * NSA_DH
    w = jnp.concatenate([cmp_w[:, :half], cmp_w[:, half:]], axis=2).astype(BF16)
    return pl.pallas_call(
        _cmp_body,
        grid=(B,),
        in_specs=[pl.BlockSpec((None, 4, n_ch, half), lambda b: (b, 0, 0, 0)),
                  pl.BlockSpec((2, half, LANES), lambda b: (0, 0, 0)),
                  pl.BlockSpec((2, 1, 2 * half), lambda b: (0, 0, 0)),
                  pl.BlockSpec((2, 2 * half, NSA_DH), lambda b: (0, 0, 0))],
        out_specs=pl.BlockSpec((None, 4, n_ch, LANES), lambda b: (b, 0, 0, 0)),
        out_shape=jax.ShapeDtypeStruct((B, 4, n_ch, LANES), F32),
        name="nsa_compress",
    )(xc, w, cmp_pos.reshape(2, 1, 2 * half), cmp_w)


def _nsa_body(qr_ref, qn_ref, kc_ref, vc_ref, kv_ref, sm_ref, e_ref, o_ref, *, tq):
    g = pl.program_id(1)
    qi = pl.program_id(2)
    lane256 = lax.broadcasted_iota(jnp.int32, (1, 2 * LANES), 1)
    lane128 = lax.broadcasted_iota(jnp.int32, (1, LANES), 1)
    lane64 = lax.broadcasted_iota(jnp.int32, (1, NSA_SEL_LEN), 1)
    row = lax.broadcasted_iota(jnp.int32, (tq, 1), 0)
    col = lax.broadcasted_iota(jnp.int32, (1, tq), 1)
    qpos = qi * tq + row
    head_lanes = [(lane256 >= NSA_DH * j) & (lane256 < NSA_DH * (j + 1)) for j in range(NSA_HPG)]
    nt_dims = (((1,), (1,)), ((), ()))

    blk = _NSA_R * (lane256 % NSA_SEL_LEN) + lane256 // NSA_SEL_LEN
    cmask = blk * NSA_CMP_STRIDE + (NSA_CMP_LEN - 1) <= qpos
    qn = qn_ref[...] * (NSA_DH ** -0.5)
    imp = jnp.zeros((tq, _NSA_NC), F32)
    o_cmp = []
    for j in range(NSA_HPG):
        qj = jnp.where(head_lanes[j], qn, 0.0).astype(BF16)
        s = lax.dot_general(qj, kc_ref[...], nt_dims, preferred_element_type=F32)
        s = jnp.where(cmask, s, -1e30)
        e = jnp.where(cmask, jnp.exp(s - jnp.max(s, axis=1, keepdims=True)), 0.0)
        p = e / jnp.maximum(jnp.sum(e, axis=1, keepdims=True), 1e-30)
        imp = imp + p
        o_cmp.append(jnp.dot(p.astype(BF16), vc_ref[...], preferred_element_type=F32))
    imps = (imp[:, 0:64] + imp[:, 64:128]) + (imp[:, 128:192] + imp[:, 192:256])

    qb = qpos // NSA_SEL_LEN
    back = qb - lane64
    score = jnp.where(back == 0, FORCE_SCORE, imps)
    score = jnp.where(back == 1, FORCE_SCORE, score)
    score = jnp.where(back == qb, FORCE_SCORE, score)
    score = jnp.where(back >= 0, score, -FORCE_SCORE)
    rank = jnp.zeros((tq, NSA_SEL_LEN), F32)
    for i in range(NSA_SEL_LEN):
        ci = score[:, i:i + 1]
        ge = jnp.where(ci >= score, 1.0, 0.0)
        gt = jnp.where(ci > score, 1.0, 0.0)
        later = jnp.where(lane64 > i, 1.0, 0.0)
        rank = rank + (gt + later * (ge - gt))
    sel = jnp.where(rank < NSA_TOPK, 1.0, 0.0) * jnp.where(back >= 0, 1.0, 0.0)
    sel = sel.astype(BF16)

    qr = qr_ref[...] * (NSA_DH ** -0.5)
    qjs = [jnp.where(head_lanes[j], qr, 0.0).astype(BF16) for j in range(NSA_HPG)]
    rows4 = NSA_HPG * tq

    def attend(t, carry, kcol, bias):
        m, l, acc = carry
        start = pl.multiple_of(t * tq, tq)
        k = kv_ref[pl.ds(start, tq), kcol:kcol + 2 * LANES]
        v = kv_ref[pl.ds(start, tq), 4 * LANES:5 * LANES]
        s = jnp.concatenate([lax.dot_general(qj, k, nt_dims, preferred_element_type=F32) + bias for qj in qjs],
                            axis=0)
        m_new = jnp.maximum(m, jnp.max(s, axis=1, keepdims=True))
        alpha = jnp.exp(m - m_new)
        p = jnp.exp(s - m_new)
        l = alpha * l + jnp.sum(p, axis=1, keepdims=True)
        acc = alpha * acc + jnp.dot(p.astype(BF16), v, preferred_element_type=F32)
        return m_new, l, acc

    def sel_bias(t, extra=None):
        on = jnp.dot(sel, e_ref[t], preferred_element_type=F32) > 0.5
        inner = 0.0 if extra is None else jnp.where(extra, 0.0, -1e30)
        return jnp.where(on, inner, -1e30)

    init = (jnp.full((rows4, 1), -1e30, F32), jnp.zeros((rows4, 1), F32), jnp.zeros((rows4, LANES), F32))
    causal = col <= row
    carry = lax.fori_loop(0, qi, lambda t, c: attend(t, c, 0, sel_bias(t)), init)
    _, l_s, acc_s = attend(qi, carry, 0, sel_bias(qi, causal))

    neg = jnp.full((tq, tq), -1e30, F32)
    carry = attend(qi, init, 2 * LANES, jnp.where(causal, 0.0, -1e30))
    carry = attend(jnp.maximum(qi - 1, 0), carry, 2 * LANES, jnp.where(qi >= 1, jnp.zeros((tq, tq), F32), neg))
    _, l_w, acc_w = attend(jnp.maximum(qi - 2, 0), carry, 2 * LANES,
                           jnp.where(qi >= 2, jnp.where(col > row, 0.0, -1e30), neg))

    sm = sm_ref[...]
    ys = []
    for j in range(NSA_HPG):
        base = 8 + 3 * (NSA_HPG * g + j)
        gate = [jax.nn.sigmoid(jnp.sum(jnp.where(lane128 == base + c, sm, 0.0), axis=1, keepdims=True))
                for c in range(3)]
        r = slice(j * tq, (j + 1) * tq)
        comb = jnp.where(lane128 < NSA_DH, acc_s[r] / l_s[r] * gate[1], acc_w[r] / l_w[r] * gate[2])
        comb = comb + o_cmp[j] * gate[0]
        ys.append(comb + pltpu.roll(comb, NSA_DH, 1))
    o_ref[:, 0:LANES] = jnp.where(lane128 < NSA_DH, ys[0], ys[1])
    o_ref[:, LANES:2 * LANES] = jnp.where(lane128 < NSA_DH, ys[2], ys[3])


def _nsa_prompt(oq, onsa_bf, osm, kc, B, S, tq=256):
    nq = S // tq
    nsb = S // NSA_SEL_LEN
    perm = jnp.transpose(kc.reshape(B, 4, nsb, _NSA_R, LANES), (0, 1, 3, 2, 4)).reshape(B, 4, _NSA_NC, LANES)
    kc_rep = jnp.tile(perm[:, 0:2, :, 0:NSA_DH], (1, 1, 1, NSA_HPG)).astype(BF16)
    vc = perm[:, 2:4].astype(BF16)
    n_kt = S // tq
    e = (jnp.arange(nsb)[None, :, None]
         == (jnp.arange(n_kt)[:, None, None] * (tq // NSA_SEL_LEN) + jnp.arange(tq)[None, None, :] // NSA_SEL_LEN))
    e = e.astype(BF16)
    return pl.pallas_call(
        functools.partial(_nsa_body, tq=tq),
        grid=(B, NSA_GROUPS, nq),
        in_specs=[pl.BlockSpec((tq, 256), lambda b, g, i: (b * nq + i, 2 + g)),
                  pl.BlockSpec((tq, 256), lambda b, g, i: (b * nq + i, 4 + g)),
                  pl.BlockSpec((None, None, _NSA_NC, 256), lambda b, g, i: (b, g, 0, 0)),
                  pl.BlockSpec((None, None, _NSA_NC, LANES), lambda b, g, i: (b, g, 0, 0)),
                  pl.BlockSpec((S, _NSA_BF_W), lambda b, g, i: (b, g)),
                  pl.BlockSpec((tq, LANES), lambda b, g, i: (b * nq + i, 0)),
                  pl.BlockSpec((n_kt, nsb, tq), lambda b, g, i: (0, 0, 0))],
        out_specs=pl.BlockSpec((tq, 256), lambda b, g, i: (b * nq + i, g)),
        out_shape=jax.ShapeDtypeStruct((B * S, 512), F32),
        compiler_params=pltpu.CompilerParams(vmem_limit_bytes=VMEM_LIMIT),
        name="nsa_attn",
    )(oq, oq, kc_rep, vc, onsa_bf, osm, e)


def _merge_body(x_ref, ya_ref, yb_ref, yc_ref, gm_ref, sc1_ref, sh1_ref, g1_ref, gf_ref, sc2_ref, sh2_ref,
                wbg_ref, bbg_ref, wbr_ref, wout_ref, wrh_ref, wrl_ref, br_ref, xo_ref, h2_ref, lg_ref):
    x = x_ref[...]
    hb = _norm_mod(x, gm_ref[...], sc1_ref[...], sh1_ref[...]).astype(BF16)
    mix = None
    for n, y_ref in enumerate((ya_ref, yb_ref, yc_ref)):
        gate = jax.nn.sigmoid(jnp.dot(hb, wbg_ref[:, n * D_MODEL:(n + 1) * D_MODEL], preferred_element_type=F32)
                              + bbg_ref[:, n * D_MODEL:(n + 1) * D_MODEL])
        proj = jnp.dot(y_ref[...].astype(BF16), wbr_ref[n], preferred_element_type=F32)
        mix = gate * proj if mix is None else mix + gate * proj
    y = jnp.dot(mix.astype(BF16), wout_ref[...], preferred_element_type=F32)
    xn = x + g1_ref[...] * y
    xo_ref[...] = xn
    h2 = _norm_mod(xn, gf_ref[...], sc2_ref[...], sh2_ref[...])
    hi = h2.astype(BF16)
    lo = (h2 - hi.astype(F32)).astype(BF16)
    h2_ref[...] = hi
    lg_ref[...] = (jnp.dot(hi, wrh_ref[...], preferred_element_type=F32)
                   + jnp.dot(lo, wrh_ref[...], preferred_element_type=F32)
                   + jnp.dot(hi, wrl_ref[...], preferred_element_type=F32) + br_ref[...])


def _merge(x, ya, yb, yc, gm, mods, gf, wbg, bbg, wbr, wout, wrh, wrl, br, tm, rows_per_seq):
    T = x.shape[0]
    nt = T // tm
    tiles_per_seq = max(rows_per_seq // tm, 1)
    sc1, sh1, g1, sc2, sh2 = mods
    R = sc1.shape[1]
    row = lambda w: pl.BlockSpec((tm, w), lambda i: (i, 0))
    mod_spec = pl.BlockSpec((None, R, D_MODEL), lambda i: (i // tiles_per_seq, 0, 0))
    const = lambda shape: pl.BlockSpec(shape, lambda i: (0,) * len(shape))
    return pl.pallas_call(
        _merge_body,
        grid=(nt,),
        in_specs=[row(D_MODEL), row(512), row(512), row(512), const((1, D_MODEL)), mod_spec, mod_spec, mod_spec,
                  const((1, D_MODEL)), mod_spec, mod_spec,
                  const((D_MODEL, 3 * D_MODEL)), const((1, 3 * D_MODEL)), const((3, BRANCH_W, D_MODEL)),
                  const((D_MODEL, D_MODEL)), const((D_MODEL, LANES)), const((D_MODEL, LANES)), const((1, LANES))],
        out_specs=[row(D_MODEL), row(D_MODEL), row(LANES)],
        out_shape=[jax.ShapeDtypeStruct((T, D_MODEL), F32), jax.ShapeDtypeStruct((T, D_MODEL), BF16),
                   jax.ShapeDtypeStruct((T, LANES), F32)],
        compiler_params=pltpu.CompilerParams(vmem_limit_bytes=VMEM_LIMIT),
        name="merge",
    )(x, ya, yb, yc, gm.reshape(1, D_MODEL), sc1, sh1, g1, gf.reshape(1, D_MODEL), sc2, sh2,
      wbg, bbg.reshape(1, 3 * D_MODEL), wbr, wout, wrh, wrl, br)


def _moe_body(te_ref, tv_ref, x_ref, wu_ref, bu_ref, wd_ref, bd_ref, o_ref, wu_s, wd_s):
    j = pl.program_id(0)
    e = te_ref[j]
    e_prev = te_ref[jnp.maximum(j - 1, 0)]

    @pl.when((j == 0) | (e != e_prev))
    def _():
        wu_s[...] = wu_ref[0].astype(BF16)
        wd_s[...] = wd_ref[0].astype(BF16)

    @pl.when(tv_ref[j] == 1)
    def _():
        uu = jnp.dot(x_ref[...], wu_s[...], preferred_element_type=F32) + bu_ref[0]
        gl = jnp.minimum(uu[:, :D_FF], SWIGLU_LIMIT)
        up = jnp.clip(uu[:, D_FF:], -SWIGLU_LIMIT, SWIGLU_LIMIT)
        act = gl * jax.nn.sigmoid(SWIGLU_ALPHA * gl) * (up + 1.0)
        o_ref[...] = jnp.dot(act.astype(BF16), wd_s[...], preferred_element_type=F32) + bd_ref[0]

    @pl.when(tv_ref[j] == 0)
    def _():
        o_ref[...] = jnp.zeros_like(o_ref)


def _moe_grouped(x_pad, tile_e, tile_valid, w_up, b_up, w_down, b_down, tm):
    n_tiles = x_pad.shape[0] // tm
    grid_spec = pltpu.PrefetchScalarGridSpec(
        num_scalar_prefetch=2,
        grid=(n_tiles,),
        in_specs=[pl.BlockSpec((tm, D_MODEL), lambda j, te, tv: (j, 0)),
                  pl.BlockSpec((1, D_MODEL, 2 * D_FF), lambda j, te, tv: (te[j], 0, 0)),
                  pl.BlockSpec((1, 1, 2 * D_FF), lambda j, te, tv: (te[j], 0, 0)),
                  pl.BlockSpec((1, D_FF, D_MODEL), lambda j, te, tv: (te[j], 0, 0)),
                  pl.BlockSpec((1, 1, D_MODEL), lambda j, te, tv: (te[j], 0, 0))],
        out_specs=pl.BlockSpec((tm, D_MODEL), lambda j, te, tv: (j, 0)),
        scratch_shapes=[pltpu.VMEM((D_MODEL, 2 * D_FF), BF16), pltpu.VMEM((D_FF, D_MODEL), BF16)],
    )
    return pl.pallas_call(
        _moe_body,
        grid_spec=grid_spec,
        out_shape=jax.ShapeDtypeStruct((x_pad.shape[0], D_MODEL), F32),
        compiler_params=pltpu.CompilerParams(vmem_limit_bytes=VMEM_LIMIT, dimension_semantics=("arbitrary",)),
        name="moe",
    )(tile_e, tile_valid, x_pad, w_up, b_up.reshape(N_EXPERTS, 1, 2 * D_FF), w_down,
      b_down.reshape(N_EXPERTS, 1, D_MODEL))


def _moe(h2, logits, w_up, b_up, w_down, b_down, tm=256):
    T = h2.shape[0]
    A = T * TOP_K
    top_v, top_i = lax.top_k(logits, TOP_K)
    wts = jax.nn.softmax(top_v, axis=-1)
    flat_e = top_i.reshape(A).astype(jnp.int32)
    order = jnp.argsort(flat_e, stable=True).astype(jnp.int32)
    sorted_e = flat_e[order]
    counts = jnp.zeros((N_EXPERTS,), jnp.int32).at[flat_e].add(1)
    tiles_per_e = (counts + tm - 1) // tm
    tile_end = jnp.cumsum(tiles_per_e)
    pad_start = (tile_end - tiles_per_e) * tm
    grp_start = jnp.cumsum(counts) - counts
    pos_sorted = pad_start[sorted_e] + (jnp.arange(A, dtype=jnp.int32) - grp_start[sorted_e])
    n_tiles = (A + N_EXPERTS * (tm - 1)) // tm + 1
    NP = n_tiles * tm
    src_tok = jnp.full((NP,), T, jnp.int32).at[pos_sorted].set(order // TOP_K)
    x_ext = jnp.concatenate([h2, jnp.zeros((1, D_MODEL), h2.dtype)], axis=0)
    x_pad = x_ext[src_tok]
    tile_idx = jnp.arange(n_tiles, dtype=jnp.int32)
    tile_e = jnp.minimum(jnp.searchsorted(tile_end, tile_idx, side='right'), N_EXPERTS - 1).astype(jnp.int32)
    tile_valid = (tile_idx < tile_end[-1]).astype(jnp.int32)
    last_e = tile_e[jnp.maximum(tile_end[-1] - 1, 0)]
    tile_e = jnp.where(tile_valid == 1, tile_e, last_e)
    y_pad = _moe_grouped(x_pad, tile_e, tile_valid, w_up, b_up, w_down, b_down, tm)
    inv_pos = jnp.zeros((A,), jnp.int32).at[order].set(pos_sorted)
    y_sel = y_pad[inv_pos].reshape(T, TOP_K, D_MODEL)
    return jnp.sum(wts[:, :, None] * y_sel, axis=1)


def _dot_hp(a, b):
    return jnp.dot(a, b, precision=lax.Precision.HIGHEST, preferred_element_type=F32)


_IN_ROPE_FLAGS = np.concatenate([np.full((w // LANES,), int(r), np.int32) for (_, w, r, _, _) in _IN_PLAN])


def _in_s_body(flag_ref, x_ref, g_ref, sc_ref, sh_ref, cos_ref, sin_ref, w_ref, o_ref):
    n = pl.program_id(0)
    h = _norm_mod(x_ref[...], g_ref[...], sc_ref[...], sh_ref[...])
    y = _dot_hp(h, w_ref[...])
    lane = lax.broadcasted_iota(jnp.int32, (1, LANES), 1)
    yr = _rope128(y, cos_ref[...], sin_ref[...], (lane % DA_QK) < (DA_QK // 2))
    o_ref[...] = jnp.where(flag_ref[n] == 1, yr, y)


def _in_proj_sample(x, g, sc, sh, cos_t, sin_t, w_packed):
    T = x.shape[0]
    full = lambda shape: pl.BlockSpec(shape, lambda n, f: (0,) * len(shape))
    grid_spec = pltpu.PrefetchScalarGridSpec(
        num_scalar_prefetch=1,
        grid=(_W_IN_PACKED // LANES,),
        in_specs=[full((T, D_MODEL)), full((1, D_MODEL)), full((T, D_MODEL)), full((T, D_MODEL)),
                  full((T, LANES)), full((T, LANES)),
                  pl.BlockSpec((D_MODEL, LANES), lambda n, f: (0, n))],
        out_specs=pl.BlockSpec((T, LANES), lambda n, f: (0, n)),
    )
    return pl.pallas_call(
        _in_s_body, grid_spec=grid_spec,
        out_shape=jax.ShapeDtypeStruct((T, _W_IN_PACKED), F32),
        name="in_proj_sample",
    )(jnp.asarray(_IN_ROPE_FLAGS), x, g.reshape(1, D_MODEL), sc, sh, cos_t, sin_t, w_packed)


def _mix_s_body(x_ref, g_ref, sc_ref, sh_ref, ya_ref, yb_ref, yc_ref, wg0, wg1, wg2, bg0, bg1, bg2, wbr_ref, o_ref):
    h = _norm_mod(x_ref[...], g_ref[...], sc_ref[...], sh_ref[...])
    mix = None
    for n, (y_ref, wg, bg) in enumerate(((ya_ref, wg0, bg0), (yb_ref, wg1, bg1), (yc_ref, wg2, bg2))):
        gate = jax.nn.sigmoid(_dot_hp(h, wg[...]) + bg[...])
        proj = _dot_hp(y_ref[...], wbr_ref[n])
        mix = gate * proj if mix is None else mix + gate * proj
    o_ref[...] = mix


def _gate_mix_sample(x, g, sc, sh, ya, yb, yc, w_bgate, b_bgate, w_branch, tn=256):
    T = x.shape[0]
    nj = D_MODEL // tn
    full = lambda shape: pl.BlockSpec(shape, lambda j: (0,) * len(shape))
    wg = [pl.BlockSpec((D_MODEL, tn), lambda j, n=n: (0, n * nj + j)) for n in range(N_BRANCH)]
    bg = [pl.BlockSpec((1, tn), lambda j, n=n: (0, n * nj + j)) for n in range(N_BRANCH)]
    bb = b_bgate.reshape(1, N_BRANCH * D_MODEL)
    return pl.pallas_call(
        _mix_s_body, grid=(nj,),
        in_specs=[full((T, D_MODEL)), full((1, D_MODEL)), full((T, D_MODEL)), full((T, D_MODEL)),
                  full((T, BRANCH_W)), full((T, BRANCH_W)), full((T, BRANCH_W))] + wg + bg
                 + [pl.BlockSpec((N_BRANCH, BRANCH_W, tn), lambda j: (0, 0, j))],
        out_specs=pl.BlockSpec((T, tn), lambda j: (0, j)),
        out_shape=jax.ShapeDtypeStruct((T, D_MODEL), F32),
        name="gate_mix_sample",
    )(x, g.reshape(1, D_MODEL), sc, sh, ya, yb, yc, w_bgate, w_bgate, w_bgate, bb, bb, bb, w_branch)


def _out_s_body(x_ref, mix_ref, g1_ref, wout_ref, gf_ref, sc2_ref, sh2_ref, wr_ref, br_ref, xo_ref, h2_ref, lg_ref):
    xn = x_ref[...] + g1_ref[...] * _dot_hp(mix_ref[...], wout_ref[...])
    xo_ref[...] = xn
    h2 = _norm_mod(xn, gf_ref[...], sc2_ref[...], sh2_ref[...])
    h2_ref[...] = h2
    lg_ref[...] = _dot_hp(h2, wr_ref[...]) + br_ref[...]


def _out_sample(x, mix, g1, w_out, gf, sc2, sh2, wr, br):
    T = x.shape[0]
    return pl.pallas_call(
        _out_s_body,
        out_shape=[jax.ShapeDtypeStruct((T, D_MODEL), F32), jax.ShapeDtypeStruct((T, D_MODEL), F32),
                   jax.ShapeDtypeStruct((T, LANES), F32)],
        compiler_params=pltpu.CompilerParams(vmem_limit_bytes=VMEM_LIMIT),
        name="out_sample",
    )(x, mix, g1, w_out, gf.reshape(1, D_MODEL), sc2, sh2, wr, br)


def _moe_s_body(h_ref, gate_ref, wu_ref, bu_ref, wd_ref, bd_ref, x_ref, g2_ref, o_ref, acc):
    e = pl.program_id(0)

    @pl.when(e == 0)
    def _():
        acc[...] = jnp.zeros_like(acc)

    uu = _dot_hp(h_ref[...], wu_ref[0]) + bu_ref[0]
    gl = jnp.minimum(uu[:, :D_FF], SWIGLU_LIMIT)
    up = jnp.clip(uu[:, D_FF:], -SWIGLU_LIMIT, SWIGLU_LIMIT)
    act = gl * jax.nn.sigmoid(SWIGLU_ALPHA * gl) * (up + 1.0)
    acc[...] += gate_ref[0] * (_dot_hp(act, wd_ref[0]) + bd_ref[0])

    @pl.when(e == N_EXPERTS - 1)
    def _():
        o_ref[...] = x_ref[...] + g2_ref[...] * acc[...]


def _moe_sample(h2, logits, w_up, b_up, w_down, b_down, x, g2):
    T = h2.shape[0]
    top_v, top_i = lax.top_k(logits, TOP_K)
    wts = jax.nn.softmax(top_v, axis=-1)
    gate = jnp.einsum('tk,tke->et', wts, jax.nn.one_hot(top_i, N_EXPERTS, dtype=F32),
                      precision=lax.Precision.HIGHEST)[:, :, None]
    full = lambda shape: pl.BlockSpec(shape, lambda e: (0,) * len(shape))
    return pl.pallas_call(
        _moe_s_body, grid=(N_EXPERTS,),
        in_specs=[full((T, D_MODEL)), pl.BlockSpec((1, T, 1), lambda e: (e, 0, 0)),
                  pl.BlockSpec((1, D_MODEL, 2 * D_FF), lambda e: (e, 0, 0)),
                  pl.BlockSpec((1, 1, 2 * D_FF), lambda e: (e, 0, 0)),
                  pl.BlockSpec((1, D_FF, D_MODEL), lambda e: (e, 0, 0)),
                  pl.BlockSpec((1, 1, D_MODEL), lambda e: (e, 0, 0)),
                  full((T, D_MODEL)), full((T, D_MODEL))],
        out_specs=full((T, D_MODEL)),
        out_shape=jax.ShapeDtypeStruct((T, D_MODEL), F32),
        scratch_shapes=[pltpu.VMEM((T, D_MODEL), F32)],
        compiler_params=pltpu.CompilerParams(vmem_limit_bytes=VMEM_LIMIT, dimension_semantics=("arbitrary",)),
        name="moe_sample",
    )(h2, gate, w_up, b_up.reshape(N_EXPERTS, 1, 2 * D_FF), w_down, b_down.reshape(N_EXPERTS, 1, D_MODEL), x, g2)


def _final_norm_body(x_ref, g_ref, o_ref):
    x = x_ref[...]
    o_ref[...] = x * lax.rsqrt(jnp.mean(x * x, axis=-1, keepdims=True) + EPS) * g_ref[...]


def _final_norm(x, g, tm):
    T = x.shape[0]
    return pl.pallas_call(
        _final_norm_body,
        grid=(T // tm,),
        in_specs=[pl.BlockSpec((tm, D_MODEL), lambda i: (i, 0)), pl.BlockSpec((1, D_MODEL), lambda i: (0, 0))],
        out_specs=pl.BlockSpec((tm, D_MODEL), lambda i: (i, 0)),
        out_shape=jax.ShapeDtypeStruct((T, D_MODEL), F32),
        name="final_norm",
    )(x, g.reshape(1, D_MODEL))


def _rms_norm(x, g):
    y = x * lax.rsqrt(jnp.mean(x * x, axis=-1, keepdims=True) + EPS)
    return y * g


def _masked_softmax(s, mask):
    s = jnp.where(mask, s, -1e30)
    m = jnp.max(s, axis=-1, keepdims=True)
    e = jnp.where(mask, jnp.exp(s - m), 0.0)
    return e / jnp.maximum(jnp.sum(e, axis=-1, keepdims=True), 1e-30)


def _sweep_queries(fn, qs, qpos, block):
    B, Q = qs[0].shape[:2]
    nb = Q // block

    def split(a):
        return jnp.moveaxis(a.reshape((B, nb, block) + a.shape[2:]), 1, 0)

    out = lax.map(lambda xs: fn(*xs[0], xs[1]), (tuple(split(a) for a in qs), qpos.reshape(nb, block)))
    out = jnp.moveaxis(out, 0, 1)
    return out.reshape((B, Q) + out.shape[3:])


def _diff_attn_core(q, k, v, qpos, kpos, lam, lam_init, g_sub):
    s = jnp.einsum('bqhcd,bkhcd->bhcqk', q, k) * (DA_QK ** -0.5)
    mask = kpos[None, :] <= qpos[:, None]
    p = _masked_softmax(s, mask)
    a = p[:, :, 0] - lam * p[:, :, 1]
    o = jnp.einsum('bhqk,bkhd->bqhd', a, v)
    return _rms_norm(o, g_sub) * (1.0 - lam_init)


def _mlstm_chunked(q, k, v, log_i, log_f, C0, n0, m0):
    B, S, H, DK = q.shape
    L = math.gcd(S, ML_CHUNK)
    nc = S // L

    def to_chunks(a):
        return jnp.moveaxis(a.reshape((B, nc, L) + a.shape[2:]), 1, 0)

    kf = k * (DK ** -0.5)
    causal = jnp.tril(jnp.ones((L, L), dtype=bool))[None, :, :, None]

    def step(carry, xs):
        C, n, m = carry
        qc, kc, vc, ic, fc = xs
        b = jnp.cumsum(fc, axis=1)
        a = b + m[:, None, :]
        D = jnp.where(causal, b[:, :, None, :] - b[:, None, :, :] + ic[:, None, :, :], -jnp.inf)
        mt = jnp.maximum(a, jnp.max(D, axis=2))
        w_inter = jnp.exp(a - mt)
        w_intra = jnp.exp(D - mt[:, :, None, :])
        sc = w_intra * jnp.einsum('bthd,bshd->btsh', qc, kc)
        num = w_inter[..., None] * jnp.einsum('bhvd,bthd->bthv', C, qc) + jnp.einsum('btsh,bshv->bthv', sc, vc)
        den = w_inter * jnp.einsum('bhd,bthd->bth', n, qc) + jnp.sum(sc, axis=2)
        h = num / jnp.maximum(jnp.abs(den), jnp.exp(-mt))[..., None]
        m_new = mt[:, -1]
        wi = w_intra[:, -1]
        wc = w_inter[:, -1]
        C_new = wc[..., None, None] * C + jnp.einsum('bsh,bshv,bshd->bhvd', wi, vc, kc)
        n_new = wc[..., None] * n + jnp.einsum('bsh,bshd->bhd', wi, kc)
        return (C_new, n_new, m_new), h

    (C1, n1, m1), hs = lax.scan(step, (C0, n0, m0),
                                (to_chunks(q), to_chunks(kf), to_chunks(v), to_chunks(log_i), to_chunks(log_f)))
    hs = jnp.moveaxis(hs, 0, 1).reshape(B, S, H, v.shape[-1])
    return hs, C1, n1, m1


def _nsa_compress(rows, pos_emb, w):
    B, T, G, dh = rows.shape
    Tp = -(-T // NSA_SEL_LEN) * NSA_SEL_LEN
    rows = jnp.pad(rows, ((0, 0), (0, Tp - T), (0, 0), (0, 0)))
    ch = rows.reshape(B, Tp // NSA_CMP_STRIDE, NSA_CMP_STRIDE, G, dh)
    n_sub = NSA_CMP_LEN // NSA_CMP_STRIDE
    nc = ch.shape[1] - n_sub + 1
    blocks = jnp.concatenate([ch[:, i:i + nc] for i in range(n_sub)], axis=2)
    blocks = blocks + pos_emb[None, None, :, None, :]
    return jnp.einsum('bnlgd,lde->bnge', blocks, w.reshape(NSA_CMP_LEN, dh, dh))


def _nsa_cmp_branch(q, kc, vc, qpos):
    B, Q, H, dh = q.shape
    nc = kc.shape[1]
    qg = q.reshape(B, Q, NSA_GROUPS, NSA_HPG, dh)
    s = jnp.einsum('bqgjd,bngd->bgjqn', qg, kc) * (dh ** -0.5)
    ends = jnp.arange(nc) * NSA_CMP_STRIDE + NSA_CMP_LEN - 1
    mask = ends[None, :] <= qpos[:, None]
    p = _masked_softmax(s, mask)
    o = jnp.einsum('bgjqn,bngd->bqgjd', p, vc).reshape(B, Q, H, dh)
    return o, jnp.sum(p, axis=2)


def _nsa_select(imp_cmp, qpos, nsb):
    r = NSA_SEL_LEN // NSA_CMP_STRIDE
    imp = jnp.pad(imp_cmp, ((0, 0), (0, 0), (0, 0), (0, nsb * r - imp_cmp.shape[-1])))
    imp = jnp.sum(imp.reshape(imp.shape[:3] + (nsb, r)), axis=-1)
    j = jnp.arange(nsb)[None, :]
    qb = (qpos // NSA_SEL_LEN)[:, None]
    forced = (j == 0) | (j == qb) | (j == qb - 1)
    score = jnp.where(forced, FORCE_SCORE, imp)
    score = jnp.where(j <= qb, score, -FORCE_SCORE)
    _, idx = lax.top_k(score, min(NSA_TOPK, nsb))
    valid = idx <= (qpos // NSA_SEL_LEN)[None, None, :, None]
    return jnp.transpose(idx, (0, 2, 1, 3)), jnp.transpose(valid, (0, 2, 1, 3))


def _to_sel_blocks(rows, nsb):
    B, T, G, dh = rows.shape
    rows = jnp.pad(rows, ((0, 0), (0, nsb * NSA_SEL_LEN - T), (0, 0), (0, 0)))
    return jnp.transpose(rows.reshape(B, nsb, NSA_SEL_LEN, G, dh), (0, 3, 1, 2, 4))


def _nsa_slc_branch(q, idx, valid, qpos, kblk, vblk):
    B, Qb, H, dh = q.shape
    bi = jnp.arange(B)[:, None, None, None]
    gi = jnp.arange(NSA_GROUPS)[None, None, :, None]
    ks = kblk[bi, gi, idx]
    vs = vblk[bi, gi, idx]
    kpos = idx[..., None] * NSA_SEL_LEN + jnp.arange(NSA_SEL_LEN)
    mask = valid[..., None] & (kpos <= qpos[None, :, None, None, None])
    qg = q.reshape(B, Qb, NSA_GROUPS, NSA_HPG, dh)
    s = jnp.einsum('bqgjd,bqgnld->bqgjnl', qg, ks) * (dh ** -0.5)
    p = _masked_softmax(s.reshape(B, Qb, NSA_GROUPS, NSA_HPG, -1), mask.reshape(B, Qb, NSA_GROUPS, 1, -1))
    o = jnp.einsum('bqgjm,bqgmd->bqgjd', p, vs.reshape(B, Qb, NSA_GROUPS, -1, dh))
    return o.reshape(B, Qb, H, dh)


def _nsa_win_branch(q, kw, vw, qpos, kpos):
    B, NB, QB, H, dh = q.shape
    qg = q.reshape(B, NB, QB, NSA_GROUPS, NSA_HPG, dh)
    s = jnp.einsum('bnqgjd,bnkgd->bngjqk', qg, kw) * (dh ** -0.5)
    dpos = qpos[:, :, None] - kpos[:, None, :]
    mask = (dpos >= 0) & (dpos < NSA_WINDOW) & (kpos[:, None, :] >= 0)
    p = _masked_softmax(s, mask[None, :, None, None])
    o = jnp.einsum('bngjqk,bnkgd->bnqgjd', p, vw)
    return o.reshape(B, NB * QB, H, dh)


def _gather_pages(cache, l, page_table):
    g = cache[l, page_table]
    return g.reshape((g.shape[0], g.shape[1] * g.shape[2]) + g.shape[3:])


def _mlstm_mixer(oml, osm, gate_b, norm_g, B, Q, past):
    qm = oml[:, 0:256].reshape(B, Q, ML_HEADS, ML_QK)
    km = oml[:, 256:512].reshape(B, Q, ML_HEADS, ML_QK)
    vm = oml[:, 512:1024].reshape(B, Q, ML_HEADS, ML_V)
    om = oml[:, 1024:1536].reshape(B, Q, ML_HEADS, ML_V)
    im = osm[:, 0:4].reshape(B, Q, ML_HEADS)
    fm = osm[:, 4:8].reshape(B, Q, ML_HEADS)
    log_i = im + gate_b[:ML_HEADS]
    log_f = jax.nn.log_sigmoid(fm + gate_b[ML_HEADS:])
    if past is None:
        C0 = jnp.zeros((B, ML_HEADS, ML_V, ML_QK), F32)
        n0 = jnp.zeros((B, ML_HEADS, ML_QK), F32)
        m0 = jnp.zeros((B, ML_HEADS), F32)
    else:
        C0, n0, m0 = past
    hm, C1, n1, m1 = _mlstm_chunked(qm, km, vm, log_i, log_f, C0, n0, m0)
    hm = _rms_norm(hm, norm_g) * jax.nn.sigmoid(om)
    return hm.reshape(B * Q, ML_HEADS * ML_V), C1, n1, m1


def _nsa_mixer(oq, onkv, owin, osm, pos, cmp_pos, cmp_w, B, Q, past):
    qn_rot = oq[:, 512:1024].reshape(B, Q, NSA_HEADS, NSA_DH)
    qn = oq[:, 1024:1536].reshape(B, Q, NSA_HEADS, NSA_DH)
    nk = onkv.reshape(B, Q, 4, NSA_GROUPS, NSA_DH)
    cmp_k, cmp_v, slc_k, slc_v = nk[:, :, 0], nk[:, :, 1], nk[:, :, 2], nk[:, :, 3]
    wk = owin.reshape(B, Q, 2, NSA_GROUPS, NSA_DH)
    win_k, win_v = wk[:, :, 0], wk[:, :, 1]
    if past is not None:
        pn, buf = past
        cmp_k = jnp.concatenate([pn[:, :, 0], cmp_k], axis=1)
        cmp_v = jnp.concatenate([pn[:, :, 1], cmp_v], axis=1)
        slc_k = jnp.concatenate([pn[:, :, 2], slc_k], axis=1)
        slc_v = jnp.concatenate([pn[:, :, 3], slc_v], axis=1)
    nsb = -(-cmp_k.shape[1] // NSA_SEL_LEN)
    kc = _nsa_compress(cmp_k, cmp_pos[0], cmp_w[0])
    vc = _nsa_compress(cmp_v, cmp_pos[1], cmp_w[1])
    o_cmp, imp = _nsa_cmp_branch(qn, kc, vc, pos)
    idx, valid = _nsa_select(imp, pos, nsb)
    kblk = _to_sel_blocks(slc_k, nsb)
    vblk = _to_sel_blocks(slc_v, nsb)
    o_slc = _sweep_queries(lambda q, ix, ok, p: _nsa_slc_branch(q, ix, ok, p, kblk, vblk),
                           (qn_rot, idx, valid), pos, math.gcd(Q, NSA_QBLOCK))
    if past is None:
        win_len = min(NSA_WINDOW, Q)
        QB = math.gcd(Q, NSA_QBLOCK)
        nb, nw = Q // QB, NSA_WINDOW // QB
        kp = jnp.pad(win_k, ((0, 0), (NSA_WINDOW, 0), (0, 0), (0, 0))).reshape(B, nb + nw, QB, NSA_GROUPS, NSA_DH)
        vp = jnp.pad(win_v, ((0, 0), (NSA_WINDOW, 0), (0, 0), (0, 0))).reshape(B, nb + nw, QB, NSA_GROUPS, NSA_DH)
        band_k = jnp.concatenate([kp[:, i:i + nb] for i in range(nw + 1)], axis=2)
        band_v = jnp.concatenate([vp[:, i:i + nb] for i in range(nw + 1)], axis=2)
        kpos_w = (jnp.arange(nb) * QB)[:, None] - NSA_WINDOW + jnp.arange((nw + 1) * QB)[None, :]
        o_win = _nsa_win_branch(qn_rot.reshape(B, nb, QB, NSA_HEADS, NSA_DH), band_k, band_v,
                                pos.reshape(nb, QB), kpos_w)
        new_win = jnp.stack([win_k, win_v], axis=2)[:, Q - win_len:]
    else:
        win_len = buf.shape[1]
        kw = jnp.concatenate([buf[:, :, 0], win_k], axis=1)
        vw = jnp.concatenate([buf[:, :, 1], win_v], axis=1)
        kpos_w = (pos[0] - win_len) + jnp.arange(win_len + Q)
        o_win = _nsa_win_branch(qn_rot[:, None], kw[:, None], vw[:, None], pos[None, :], kpos_w[None, :])
        new_win = jnp.stack([kw, vw], axis=2)[:, -win_len:]
    gn = jax.nn.sigmoid(osm[:, 8:32].reshape(B, Q, NSA_HEADS, 3))
    yc = (gn[..., 0:1] * o_cmp + gn[..., 1:2] * o_slc + gn[..., 2:3] * o_win).reshape(B * Q, NSA_HEADS * NSA_DH)
    return yc, new_win


def _diff_attn_sample(oq, odkv, pd, pos, da_lam, g_sub, lam_init, B, Q):
    qa = oq[:, 0:512].reshape(B, Q, DA_HEADS, 2, DA_QK)
    ka = odkv[:, 0:512].reshape(B, Q, DA_HEADS, 2, DA_QK)
    va = odkv[:, 512:1024].reshape(B, Q, DA_HEADS, DA_V)
    k_all = jnp.concatenate([pd[:, :, 0].reshape(B, -1, DA_HEADS, 2, DA_QK), ka], axis=1)
    v_all = jnp.concatenate([pd[:, :, 1], va], axis=1)
    kpos = jnp.arange(k_all.shape[1])
    lam = jnp.exp(jnp.sum(da_lam[0] * da_lam[1])) - jnp.exp(jnp.sum(da_lam[2] * da_lam[3])) + lam_init
    ya = _diff_attn_core(qa, k_all, v_all, pos, kpos, lam, lam_init, g_sub)
    return ya.reshape(B * Q, DA_HEADS * DA_V)


def kernel(x_prompt, x_sample, cache_diff_kv, cache_nsa_kv, state_nsa_win, state_mlstm_C, state_mlstm_n, state_mlstm_m, page_table, c_prompt, c_sample, norm_mix_g, norm_ffn_g, w_ada, b_ada, w_in, da_lam, da_subln_g, ml_gate_b, ml_norm_g, nsa_cmp_pos, nsa_cmp_w, w_branch, w_bgate, b_bgate, w_out, w_router, b_router, w_up, b_up, w_down, b_down, final_g):
    B, S, D = x_prompt.shape
    Bs, Qs, _ = x_sample.shape
    Tp, Ts = B * S, Bs * Qs
    tm_p = 256
    pos_p = jnp.arange(S)
    pos_s = PAST_LEN + jnp.arange(Qs)
    cos_p, sin_p = _rope_tables(pos_p)
    cos_s, sin_s = _rope_tables(jnp.broadcast_to(pos_s[None, :], (Bs, Qs)).reshape(Ts))

    mods = _ada_all(jnp.concatenate([c_prompt, c_sample], axis=0), w_ada, b_ada)

    xp = x_prompt.reshape(Tp, D)
    xs = x_sample.reshape(Ts, D)
    sp = {k: [] for k in ('diff', 'nsa', 'win', 'C', 'n', 'm')}
    ss = {k: [] for k in ('diff', 'nsa', 'win', 'C', 'n', 'm')}
    for l in range(DEPTH):
        lam_init = 0.8 - 0.6 * math.exp(-0.3 * l)
        w_packed = _pack_w_in_ext(w_in[l])
        wbg = w_bgate[l].astype(BF16)
        wbr = w_branch[l].astype(BF16)
        wout = w_out[l].astype(BF16)
        wr = jnp.pad(w_router[l], ((0, 0), (0, LANES - N_EXPERTS)))
        wrh = wr.astype(BF16)
        wrl = (wr - wrh.astype(F32)).astype(BF16)
        br = jnp.pad(b_router[l], (0, LANES - N_EXPERTS)).reshape(1, LANES)
        mod_p = [m.reshape(B, 1, D) for m in jnp.split(mods[l, :B], 6, axis=-1)]
        mod_s = [m.reshape(1, Ts, D) for m in jnp.split(mods[l, B:], 6, axis=-1)]

        oq, odkv, onkv, owin, oml, osm, odkv_bf, onsa_bf = _in_proj(xp, norm_mix_g[l], mod_p[1], mod_p[0], cos_p, sin_p,
                                                           w_packed, tm_p, S)
        ya = _diff_attn_prompt(oq, odkv_bf, da_lam[l], da_subln_g[l], lam_init, B, S)
        yb, C1, n1, m1 = _mlstm_mixer(oml, osm, ml_gate_b[l], ml_norm_g[l], B, S, None)
        kc = _nsa_compress_prompt(onkv, nsa_cmp_pos[l], nsa_cmp_w[l], B, S)
        yc = _nsa_prompt(oq, onsa_bf, osm, kc, B, S)
        new_win = owin.reshape(B, S, 2, NSA_GROUPS, NSA_DH)[:, S - min(NSA_WINDOW, S):]
        xp, h2p, lgp = _merge(xp, ya, yb, yc, norm_mix_g[l], (mod_p[1], mod_p[0], mod_p[2], mod_p[4], mod_p[3]),
                              norm_ffn_g[l], wbg, b_bgate[l], wbr, wout, wrh, wrl, br, tm_p, S)
        sp['diff'].append(odkv.reshape(B, S, 2, DA_HEADS, DA_V))
        sp['nsa'].append(onkv.reshape(B, S, 4, NSA_GROUPS, NSA_DH))
        sp['win'].append(new_win)
        sp['C'].append(C1); sp['n'].append(n1); sp['m'].append(m1)

        sc1s, sh1s = mod_s[1].reshape(Ts, D), mod_s[0].reshape(Ts, D)
        u = _in_proj_sample(xs, norm_mix_g[l], sc1s, sh1s, cos_s, sin_s, _pack_w_in(w_in[l], F32))
        oq, odkv, onkv, owin, oml, osm = jnp.split(u, [int(c) for c in np.cumsum(_IN_OUT_W)[:-1]], axis=1)
        with jax.default_matmul_precision("highest"):
            pd = _gather_pages(cache_diff_kv, l, page_table)
            pn = _gather_pages(cache_nsa_kv, l, page_table)
            ya = _diff_attn_sample(oq, odkv, pd, pos_s, da_lam[l], da_subln_g[l], lam_init, Bs, Qs)
            yb, C1, n1, m1 = _mlstm_mixer(oml, osm, ml_gate_b[l], ml_norm_g[l], Bs, Qs,
                                          (state_mlstm_C[l], state_mlstm_n[l], state_mlstm_m[l]))
            yc, new_win = _nsa_mixer(oq, onkv, owin, osm, pos_s, nsa_cmp_pos[l], nsa_cmp_w[l], Bs, Qs,
                                     (pn, state_nsa_win[l]))
        mix = _gate_mix_sample(xs, norm_mix_g[l], sc1s, sh1s, ya, yb, yc, w_bgate[l], b_bgate[l], w_branch[l])
        xs, h2s, lgs = _out_sample(xs, mix, mod_s[2].reshape(Ts, D), w_out[l], norm_ffn_g[l],
                                   mod_s[4].reshape(Ts, D), mod_s[3].reshape(Ts, D), wr, br)
        xs = _moe_sample(h2s, lgs[:, :N_EXPERTS], w_up[l], b_up[l], w_down[l], b_down[l], xs,
                         mod_s[5].reshape(Ts, D))
        ss['diff'].append(odkv.reshape(Bs, Qs, 2, DA_HEADS, DA_V))
        ss['nsa'].append(onkv.reshape(Bs, Qs, 4, NSA_GROUPS, NSA_DH))
        ss['win'].append(new_win)
        ss['C'].append(C1); ss['n'].append(n1); ss['m'].append(m1)

        ym = _moe(h2p, lgp[:, :N_EXPERTS], w_up[l], b_up[l], w_down[l], b_down[l])
        xp = xp + jnp.repeat(mod_p[5].reshape(B, D), S, axis=0) * ym

    y_prompt = _final_norm(xp, final_g, tm_p).reshape(B, S, D)
    y_sample = _final_norm(xs, final_g, Ts).reshape(Bs, Qs, D)
    stk = lambda d, k: jnp.stack(d[k], axis=0)
    return (y_prompt, y_sample,
            stk(sp, 'diff'), stk(sp, 'nsa'), stk(sp, 'win'), stk(sp, 'C'), stk(sp, 'n'), stk(sp, 'm'),
            stk(ss, 'diff'), stk(ss, 'nsa'), stk(ss, 'win'), stk(ss, 'C'), stk(ss, 'n'), stk(ss, 'm'))
```

```python
import functools
import math

import jax
import jax.numpy as jnp
import numpy as np
from jax import lax
from jax.experimental import pallas as pl
from jax.experimental.pallas import tpu as pltpu

D_MODEL = 1024
DEPTH = 4
PAST_LEN = 8192
EPS = 1e-6
ROPE_THETA = 10000.0
BRANCH_W = 512
N_BRANCH = 3
DA_HEADS = 4
DA_QK = 64
DA_V = 128
ML_HEADS = 4
ML_QK = 64
ML_V = 128
ML_CHUNK = 64
NSA_HEADS = 8
NSA_GROUPS = 2
NSA_HPG = NSA_HEADS // NSA_GROUPS
NSA_DH = 64
NSA_CMP_LEN = 32
NSA_CMP_STRIDE = 16
NSA_SEL_LEN = 64
NSA_TOPK = 16
NSA_WINDOW = 512
NSA_QBLOCK = 64
FORCE_SCORE = 1.0e4
N_EXPERTS = 32
TOP_K = 4
D_FF = 1024
SWIGLU_LIMIT = 7.0
SWIGLU_ALPHA = 1.702

LANES = 128
VMEM_LIMIT = 56 * 1024 * 1024
F32 = jnp.float32
BF16 = jnp.bfloat16

_C_QA, _C_KA, _C_VA, _C_QM, _C_KM, _C_VM, _C_OM, _C_IM, _C_FM, _C_QN, _C_KVN, _C_GN = (
    0, 512, 1024, 1536, 1792, 2048, 2560, 3072, 3076, 3080, 3592, 4360)

_IN_OUT_W = (1536, 1024, 512, 256, 1536, 128)
_W_IN_PACKED = 4992


def _in_plan():
    plan = []
    col = 0

    def add(width, rope, oi, oc):
        nonlocal col
        step = 256 if width % 256 == 0 else 128
        for s in range(0, width, step):
            plan.append((col + s, step, rope, oi, oc + s))
        col += width

    add(512, True, 0, 0)
    add(512, True, 0, 512)
    add(512, False, 0, 1024)
    add(512, True, 1, 0)
    add(512, False, 1, 512)
    add(128, False, 2, 0)
    add(128, False, 2, 128)
    add(128, True, 2, 256)
    add(128, False, 2, 384)
    add(128, True, 3, 0)
    add(128, False, 3, 128)
    add(256, False, 4, 0)
    add(256, False, 4, 256)
    add(512, False, 4, 512)
    add(512, False, 4, 1024)
    add(128, False, 5, 0)
    assert col == _W_IN_PACKED
    return tuple(plan)


_IN_PLAN = _in_plan()

_NSA_BF_W = 640
_W_IN_EXT = _W_IN_PACKED + NSA_GROUPS * _NSA_BF_W
_IN_PLAN_EXT = tuple(
    (_W_IN_PACKED + g * _NSA_BF_W + c, 128 if c == 512 else 256, c < 512, 6, g * _NSA_BF_W + c)
    for g in range(NSA_GROUPS) for c in (0, 256, 512))


def _pack_w_in_ext(w_in):
    kvn = lambda j, g: w_in[:, _C_KVN + 128 * j + 64 * g:_C_KVN + 128 * j + 64 * (g + 1)].astype(BF16)
    cols = [_pack_w_in(w_in, BF16)]
    for g in range(NSA_GROUPS):
        cols += [kvn(2, g)] * NSA_HPG + [kvn(4, g)] * NSA_HPG + [kvn(3, g), kvn(5, g)]
    return jnp.concatenate(cols, axis=1)


def _pack_w_in(w_in, dtype):
    kvn = lambda j: w_in[:, _C_KVN + 128 * j:_C_KVN + 128 * (j + 1)]
    qn = w_in[:, _C_QN:_C_QN + 512]
    small = jnp.concatenate([w_in[:, _C_IM:_C_IM + 8], w_in[:, _C_GN:_C_GN + 24],
                             jnp.zeros((w_in.shape[0], 96), w_in.dtype)], axis=1)
    cols = [w_in[:, _C_QA:_C_QA + 512], qn, qn, w_in[:, _C_KA:_C_KA + 512], w_in[:, _C_VA:_C_VA + 512],
            kvn(0), kvn(1), kvn(2), kvn(3), kvn(4), kvn(5),
            w_in[:, _C_QM:_C_QM + 256], w_in[:, _C_KM:_C_KM + 256], w_in[:, _C_VM:_C_VM + 512],
            w_in[:, _C_OM:_C_OM + 512], small]
    return jnp.concatenate(cols, axis=1).astype(dtype)


def _rope_tables(pos):
    half = DA_QK // 2
    inv = ROPE_THETA ** (-jnp.arange(half, dtype=F32) / half)
    ang = pos.astype(F32)[:, None] * inv[None, :]
    cos, sin = jnp.cos(ang), jnp.sin(ang)
    cos_t = jnp.concatenate([cos, cos, cos, cos], axis=1)
    sin_t = jnp.concatenate([-sin, sin, -sin, sin], axis=1)
    return cos_t, sin_t


def _norm_mod(x, g, sc, sh):
    y = x * lax.rsqrt(jnp.mean(x * x, axis=-1, keepdims=True) + EPS)
    return (y * g) * (1.0 + sc) + sh


def _ada_body(c_ref, w_ref, b_ref, o_ref):
    c = c_ref[...]
    s = c * jax.nn.sigmoid(c)
    o_ref[0] = jnp.dot(s, w_ref[0], precision=lax.Precision.HIGHEST, preferred_element_type=F32) + b_ref[0]


def _ada_all(c_all, w_ada, b_ada):
    R = c_all.shape[0]
    tn = 512
    return pl.pallas_call(
        _ada_body,
        grid=(DEPTH, 6 * D_MODEL // tn),
        in_specs=[pl.BlockSpec((R, D_MODEL), lambda l, n: (0, 0)),
                  pl.BlockSpec((1, D_MODEL, tn), lambda l, n: (l, 0, n)),
                  pl.BlockSpec((1, 1, tn), lambda l, n: (l, 0, n))],
        out_specs=pl.BlockSpec((1, R, tn), lambda l, n: (l, 0, n)),
        out_shape=jax.ShapeDtypeStruct((DEPTH, R, 6 * D_MODEL), F32),
        compiler_params=pltpu.CompilerParams(vmem_limit_bytes=VMEM_LIMIT),
        name="ada",
    )(c_all, w_ada, b_ada.reshape(DEPTH, 1, 6 * D_MODEL))


def _rope128(y, cos, sin_signed, first_half):
    fwd = pltpu.roll(y, LANES - DA_QK // 2, 1)
    bwd = pltpu.roll(y, DA_QK // 2, 1)
    return y * cos + jnp.where(first_half, fwd, bwd) * sin_signed


def _in_body(x_ref, g_ref, sc_ref, sh_ref, cos_ref, sin_ref, w_ref, oq, odkv, onkv, owin, oml, osm, odkv_bf, onsa_bf):
    outs = (oq, odkv, onkv, owin, oml, osm, onsa_bf)
    h = _norm_mod(x_ref[...], g_ref[...], sc_ref[...], sh_ref[...])
    hb = h.astype(BF16)
    cos = cos_ref[...]
    sin = sin_ref[...]
    lane = lax.broadcasted_iota(jnp.int32, (1, LANES), 1)
    first_half = (lane % DA_QK) < (DA_QK // 2)
    for (c0, width, rope, oi, oc) in _IN_PLAN + _IN_PLAN_EXT:
        y = jnp.dot(hb, w_ref[:, c0:c0 + width], preferred_element_type=F32)
        for s in range(0, width, LANES):
            ys = y[:, s:s + LANES]
            if rope:
                ys = _rope128(ys, cos, sin, first_half)
            outs[oi][:, oc + s:oc + s + LANES] = ys.astype(outs[oi].dtype)
            if oi == 1:
                odkv_bf[:, oc + s:oc + s + LANES] = ys.astype(BF16)


def _in_proj(x, g, sc, sh, cos_t, sin_t, w_packed, tm, rows_per_seq):
    T = x.shape[0]
    nt = T // tm
    tiles_per_seq = max(rows_per_seq // tm, 1)
    n_pos_tiles = cos_t.shape[0] // tm
    R = sc.shape[1]
    mod_spec = pl.BlockSpec((None, R, D_MODEL), lambda i: (i // tiles_per_seq, 0, 0))
    tab_spec = pl.BlockSpec((tm, LANES), lambda i: (i % n_pos_tiles, 0))
    widths = _IN_OUT_W
    bf_widths = (1024, NSA_GROUPS * _NSA_BF_W)
    out_shape = ([jax.ShapeDtypeStruct((T, w), F32) for w in widths]
                 + [jax.ShapeDtypeStruct((T, w), BF16) for w in bf_widths])
    out_specs = [pl.BlockSpec((tm, w), lambda i: (i, 0)) for w in widths + bf_widths]
    return pl.pallas_call(
        _in_body,
        grid=(nt,),
        in_specs=[pl.BlockSpec((tm, D_MODEL), lambda i: (i, 0)),
                  pl.BlockSpec((1, D_MODEL), lambda i: (0, 0)),
                  mod_spec, mod_spec, tab_spec, tab_spec,
                  pl.BlockSpec((D_MODEL, _W_IN_EXT), lambda i: (0, 0))],
        out_specs=out_specs,
        out_shape=out_shape,
        compiler_params=pltpu.CompilerParams(vmem_limit_bytes=VMEM_LIMIT),
        name="in_proj",
    )(x, g.reshape(1, D_MODEL), sc, sh, cos_t, sin_t, w_packed)


def _diff_body(q_ref, kv_ref, lam_ref, g_ref, o_ref, *, lam_init, tq):
    qi = pl.program_id(1)
    lv = lam_ref[...]
    lam = (jnp.exp(jnp.sum(lv[0:1] * lv[1:2], axis=1, keepdims=True))
           - jnp.exp(jnp.sum(lv[2:3] * lv[3:4], axis=1, keepdims=True)) + lam_init)
    lane = lax.broadcasted_iota(jnp.int32, (1, LANES), 1)
    row = lax.broadcasted_iota(jnp.int32, (2 * tq, 1), 0) % tq
    col = lax.broadcasted_iota(jnp.int32, (1, tq), 1)
    causal = col <= row
    for h in range(DA_HEADS):
        qh = q_ref[:, h * LANES:(h + 1) * LANES] * (DA_QK ** -0.5)
        qs = jnp.concatenate([jnp.where(lane < DA_QK, qh, 0.0), jnp.where(lane >= DA_QK, qh, 0.0)],
                             axis=0).astype(BF16)

        def step(kv, carry, masked):
            m, l, acc = carry
            start = pl.multiple_of(kv * tq, tq)
            k = kv_ref[pl.ds(start, tq), h * LANES:(h + 1) * LANES]
            v = kv_ref[pl.ds(start, tq), (DA_HEADS + h) * LANES:(DA_HEADS + h + 1) * LANES]
            s = lax.dot_general(qs, k, (((1,), (1,)), ((), ())), preferred_element_type=F32)
            if masked:
                s = jnp.where(causal, s, -1e30)
            m_new = jnp.maximum(m, jnp.max(s, axis=1, keepdims=True))
            alpha = jnp.exp(m - m_new)
            p = jnp.exp(s - m_new)
            l = alpha * l + jnp.sum(p, axis=1, keepdims=True)
            acc = alpha * acc + jnp.dot(p.astype(BF16), v, preferred_element_type=F32)
            return m_new, l, acc

        init = (jnp.full((2 * tq, 1), -1e30, F32), jnp.zeros((2 * tq, 1), F32), jnp.zeros((2 * tq, LANES), F32))
        carry = lax.fori_loop(0, qi, lambda kv, c: step(kv, c, False), init)
        m, l, acc = step(qi, carry, True)
        o = acc / l
        o = o[:tq] - lam * o[tq:]
        o = o * lax.rsqrt(jnp.mean(o * o, axis=-1, keepdims=True) + EPS) * g_ref[...]
        o_ref[:, h * LANES:(h + 1) * LANES] = o * (1.0 - lam_init)


def _diff_attn_prompt(oq, odkv_bf, da_lam, g_sub, lam_init, B, S, tq=256):
    nq = S // tq
    return pl.pallas_call(
        functools.partial(_diff_body, lam_init=lam_init, tq=tq),
        grid=(B, nq),
        in_specs=[pl.BlockSpec((tq, 512), lambda b, i: (b * nq + i, 0)),
                  pl.BlockSpec((S, 1024), lambda b, i: (b, 0)),
                  pl.BlockSpec((4, DA_QK), lambda b, i: (0, 0)),
                  pl.BlockSpec((1, DA_V), lambda b, i: (0, 0))],
        out_specs=pl.BlockSpec((tq, 512), lambda b, i: (b * nq + i, 0)),
        out_shape=jax.ShapeDtypeStruct((B * S, 512), F32),
        compiler_params=pltpu.CompilerParams(vmem_limit_bytes=VMEM_LIMIT),
        name="diff_attn",
    )(oq, odkv_bf, da_lam, g_sub.reshape(1, DA_V))


_DEC_PAGES = 8


def _diff_dec_body(pt_ref, q_ref, kvn_ref, lam_ref, g_ref, *rest, lam_init, n_steps):
    pages = rest[:_DEC_PAGES]
    o_ref, m_s, l_s, acc_s = rest[_DEC_PAGES:]
    j = pl.program_id(1)
    lane = lax.broadcasted_iota(jnp.int32, (1, LANES), 1)
    maps = (lane < DA_QK, lane >= DA_QK)

    @pl.when(j == 0)
    def _():
        m_s[...] = jnp.full_like(m_s, -1e30)
        l_s[...] = jnp.zeros_like(l_s)
        acc_s[...] = jnp.zeros_like(acc_s)

    def update(r, s, v):
        m_old = m_s[r:r + 1, 0:1]
        m_new = jnp.maximum(m_old, jnp.max(s, axis=0, keepdims=True))
        alpha = jnp.exp(m_old - m_new)
        p = jnp.exp(s - m_new)
        l_s[r:r + 1, :] = alpha * l_s[r:r + 1, :] + jnp.sum(p, axis=0, keepdims=True)
        acc_s[r:r + 1, :] = alpha * acc_s[r:r + 1, :] + jnp.sum(p * v, axis=0, keepdims=True)
        m_s[r:r + 1, :] = jnp.broadcast_to(m_new, (1, LANES))

    qs = [q_ref[:, h * LANES:(h + 1) * LANES] * (DA_QK ** -0.5) for h in range(DA_HEADS)]
    for pg in pages:
        for h in range(DA_HEADS):
            prod = pg[:, 0, h, :] * qs[h]
            v = pg[:, 1, h, :]
            for c in range(2):
                s = jnp.sum(jnp.where(maps[c], prod, 0.0), axis=1, keepdims=True)
                update(2 * h + c, s, v)

    @pl.when(j == n_steps - 1)
    def _():
        lv = lam_ref[...]
        lam = (jnp.exp(jnp.sum(lv[0:1] * lv[1:2], axis=1, keepdims=True))
               - jnp.exp(jnp.sum(lv[2:3] * lv[3:4], axis=1, keepdims=True)) + lam_init)
        for h in range(DA_HEADS):
            prod = kvn_ref[:, h * LANES:(h + 1) * LANES] * qs[h]
            v = kvn_ref[:, (DA_HEADS + h) * LANES:(DA_HEADS + h + 1) * LANES]
            o = []
            for c in range(2):
                r = 2 * h + c
                update(r, jnp.sum(jnp.where(maps[c], prod, 0.0), axis=1, keepdims=True), v)
                o.append(acc_s[r:r + 1, :] / l_s[r:r + 1, :])
            d = o[0] - lam * o[1]
            d = d * lax.rsqrt(jnp.mean(d * d, axis=-1, keepdims=True) + EPS) * g_ref[...]
            o_ref[:, h * LANES:(h + 1) * LANES] = d * (1.0 - lam_init)


def _diff_attn_sample_paged(oq, odkv, cache_diff_kv, page_table, l, da_lam, g_sub, lam_init):
    Bs = oq.shape[0]
    n_pages = page_table.shape[1]
    page = cache_diff_kv.shape[2]
    n_steps = n_pages // _DEC_PAGES
    page_specs = [pl.BlockSpec((None, None, page, 2, DA_HEADS, DA_V),
                               lambda b, j, pt, r=r: (l, pt[b, j * _DEC_PAGES + r], 0, 0, 0, 0))
                  for r in range(_DEC_PAGES)]
    grid_spec = pltpu.PrefetchScalarGridSpec(
        num_scalar_prefetch=1,
        grid=(Bs, n_steps),
        in_specs=[pl.BlockSpec((None, 1, 512), lambda b, j, pt: (b, 0, 0)),
                  pl.BlockSpec((None, 1, 1024), lambda b, j, pt: (b, 0, 0)),
                  pl.BlockSpec((4, DA_QK), lambda b, j, pt: (0, 0)),
                  pl.BlockSpec((1, DA_V), lambda b, j, pt: (0, 0))] + page_specs,
        out_specs=pl.BlockSpec((None, 1, 512), lambda b, j, pt: (b, 0, 0)),
        scratch_shapes=[pltpu.VMEM((2 * DA_HEADS, LANES), F32)] * 3,
    )
    out = pl.pallas_call(
        functools.partial(_diff_dec_body, lam_init=lam_init, n_steps=n_steps),
        grid_spec=grid_spec,
        out_shape=jax.ShapeDtypeStruct((Bs, 1, 512), F32),
        compiler_params=pltpu.CompilerParams(vmem_limit_bytes=VMEM_LIMIT,
                                             dimension_semantics=("arbitrary", "arbitrary")),
        name="diff_attn_sample",
    )(page_table, oq[:, 0:512].reshape(Bs, 1, 512), odkv.reshape(Bs, 1, 1024), da_lam, g_sub.reshape(1, DA_V),
      *([cache_diff_kv] * _DEC_PAGES))
    return out.reshape(Bs, 512)


_ML_L = 128


def _mlstm_body(ml_ref, sm_ref, vt_ref, gb_ref, gn_ref, y_ref, c_out, n_out, m_out, C_s, n_s, m_s, *, nc):
    j = pl.program_id(1)
    L = _ML_L

    @pl.when(j == 0)
    def _():
        C_s[...] = jnp.zeros_like(C_s)
        n_s[...] = jnp.zeros_like(n_s)
        m_s[...] = jnp.zeros_like(m_s)

    rowi = lax.broadcasted_iota(jnp.int32, (L, 1), 0)
    coli = lax.broadcasted_iota(jnp.int32, (1, L), 1)
    lane = lax.broadcasted_iota(jnp.int32, (1, LANES), 1)
    tri = jnp.where(coli <= rowi, 1.0, 0.0)
    smb = sm_ref[...] + gb_ref[...]
    nt_dims = (((1,), (1,)), ((), ()))
    for h in range(ML_HEADS):
        log_i = jnp.sum(jnp.where(lane == h, smb, 0.0), axis=1, keepdims=True)
        log_f = jax.nn.log_sigmoid(jnp.sum(jnp.where(lane == ML_HEADS + h, smb, 0.0), axis=1, keepdims=True))
        gmat = jnp.where(rowi > coli, log_f, 0.0) + jnp.where(rowi == coli, log_i, 0.0)
        d = jnp.dot(tri, gmat, precision=lax.Precision.HIGHEST, preferred_element_type=F32)
        b = d[:, 0:1] - log_i[0:1, :] + log_f[0:1, :]
        m_prev = m_s[h][:, 0:1]
        a = b + m_prev
        d = jnp.where(coli <= rowi, d, -1e30)
        mt = jnp.maximum(a, jnp.max(d, axis=1, keepdims=True))
        w_inter = jnp.exp(a - mt)
        w_intra = jnp.exp(d - mt)
        q = ml_ref[:, h * ML_QK:(h + 1) * ML_QK]
        k = ml_ref[:, 256 + h * ML_QK:256 + (h + 1) * ML_QK] * (ML_QK ** -0.5)
        v = ml_ref[:, 512 + h * ML_V:512 + (h + 1) * ML_V]
        qb, kb = q.astype(BF16), k.astype(BF16)
        sc = w_intra * lax.dot_general(qb, kb, nt_dims, preferred_element_type=F32)
        C = C_s[h]
        n = n_s[h]
        num = (w_inter * lax.dot_general(qb, C.astype(BF16), nt_dims, preferred_element_type=F32)
               + jnp.dot(sc.astype(BF16), v.astype(BF16), preferred_element_type=F32))
        den = w_inter * jnp.sum(q * n, axis=1, keepdims=True) + jnp.sum(sc, axis=1, keepdims=True)
        hh = num / jnp.maximum(jnp.abs(den), jnp.exp(-mt))
        m_new = mt[L - 1:L, :]
        wc = w_inter[L - 1:L, :]
        wi_row = w_intra[L - 1:L, :]
        wi_col = jnp.exp(b[L - 1:L, :] - b + log_i - m_new)
        vt = vt_ref[h * ML_V:(h + 1) * ML_V, :]
        C_s[h] = wc * C + jnp.dot((vt * wi_row).astype(BF16), kb, preferred_element_type=F32)
        n_s[h] = wc * n + jnp.sum(wi_col * k, axis=0, keepdims=True)
        m_s[h] = jnp.broadcast_to(m_new, (1, LANES))
        om = ml_ref[:, 1024 + h * ML_V:1024 + (h + 1) * ML_V]
        y = hh * lax.rsqrt(jnp.mean(hh * hh, axis=-1, keepdims=True) + EPS) * gn_ref[...]
        y_ref[:, h * ML_V:(h + 1) * ML_V] = y * jax.nn.sigmoid(om)

    @pl.when(j == nc - 1)
    def _():
        c_out[...] = C_s[...]
        n_out[...] = n_s[...]
        m_out[...] = m_s[...]


def _mlstm_prompt(oml, osm, gate_b, norm_g, B, S):
    nc = S // _ML_L
    vt = jnp.transpose(oml[:, 512:1024])
    gb = jnp.pad(gate_b, (0, LANES - 2 * ML_HEADS)).reshape(1, LANES)
    y, C1, n1, m1 = pl.pallas_call(
        functools.partial(_mlstm_body, nc=nc),
        grid=(B, nc),
        in_specs=[pl.BlockSpec((_ML_L, 1536), lambda b, j: (b * nc + j, 0)),
                  pl.BlockSpec((_ML_L, LANES), lambda b, j: (b * nc + j, 0)),
                  pl.BlockSpec((ML_HEADS * ML_V, _ML_L), lambda b, j: (0, b * nc + j)),
                  pl.BlockSpec((1, LANES), lambda b, j: (0, 0)),
                  pl.BlockSpec((1, ML_V), lambda b, j: (0, 0))],
        out_specs=[pl.BlockSpec((_ML_L, ML_HEADS * ML_V), lambda b, j: (b * nc + j, 0)),
                   pl.BlockSpec((None, ML_HEADS, ML_V, ML_QK), lambda b, j: (b, 0, 0, 0)),
                   pl.BlockSpec((None, ML_HEADS, 1, ML_QK), lambda b, j: (b, 0, 0, 0)),
                   pl.BlockSpec((None, ML_HEADS, 1, LANES), lambda b, j: (b, 0, 0, 0))],
        out_shape=[jax.ShapeDtypeStruct((B * S, ML_HEADS * ML_V), F32),
                   jax.ShapeDtypeStruct((B, ML_HEADS, ML_V, ML_QK), F32),
                   jax.ShapeDtypeStruct((B, ML_HEADS, 1, ML_QK), F32),
                   jax.ShapeDtypeStruct((B, ML_HEADS, 1, LANES), F32)],
        scratch_shapes=[pltpu.VMEM((ML_HEADS, ML_V, ML_QK), F32), pltpu.VMEM((ML_HEADS, 1, ML_QK), F32),
                        pltpu.VMEM((ML_HEADS, 1, LANES), F32)],
        compiler_params=pltpu.CompilerParams(dimension_semantics=("arbitrary", "arbitrary")),
        name="mlstm",
    )(oml, osm, vt, gb, norm_g.reshape(1, ML_V))
    return y, C1, n1.reshape(B, ML_HEADS, ML_QK), m1[:, :, 0, 0]


_NSA_NC = 256
_NSA_R = NSA_SEL_LEN // NSA_CMP_STRIDE


def _cmp_body(xc_ref, w_ref, pos_ref, wfull_ref, o_ref):
    row = lax.broadcasted_iota(jnp.int32, (_NSA_NC, 1), 0)
    lane = lax.broadcasted_iota(jnp.int32, (1, LANES), 1)
    for kind in range(2):
        const = jnp.dot(pos_ref[kind], wfull_ref[kind], precision=lax.Precision.HIGHEST,
                        preferred_element_type=F32)
        const = jnp.concatenate([const, jnp.zeros_like(const)], axis=1)
        for g in range(NSA_GROUPS):
            pq = jnp.dot(xc_ref[2 * kind + g], w_ref[kind], preferred_element_type=F32)
            nxt = pltpu.roll(pltpu.roll(pq, _NSA_NC - 1, 0), NSA_DH, 1)
            y = pq + nxt + const
            o_ref[2 * kind + g] = jnp.where((row < _NSA_NC - 1) & (lane < NSA_DH), y, 0.0)


def _nsa_compress_prompt(onkv, cmp_pos, cmp_w, B, S):
    n_ch = S // NSA_CMP_STRIDE
    xc = onkv[:, 0:256].reshape(B, n_ch, NSA_CMP_STRIDE, 4, NSA_DH)
    xc = jnp.transpose(xc, (0, 3, 1, 2, 4)).reshape(B, 4, n_ch, NSA_CMP_STRIDE * NSA_DH).astype(BF16)
    half = NSA_CMP_STRIDE * NSA_DH
    w = jnp.concatenate([cmp_w[:, :half], cmp_w[:, half:]], axis=2).astype(BF16)
    return pl.pallas_call(
        _cmp_body,
        grid=(B,),
        in_specs=[pl.BlockSpec((None, 4, n_ch, half), lambda b: (b, 0, 0, 0)),
                  pl.BlockSpec((2, half, LANES), lambda b: (0, 0, 0)),
                  pl.BlockSpec((2, 1, 2 * half), lambda b: (0, 0, 0)),
                  pl.BlockSpec((2, 2 * half, NSA_DH), lambda b: (0, 0, 0))],
        out_specs=pl.BlockSpec((None, 4, n_ch, LANES), lambda b: (b, 0, 0, 0)),
        out_shape=jax.ShapeDtypeStruct((B, 4, n_ch, LANES), F32),
        name="nsa_compress",
    )(xc, w, cmp_pos.reshape(2, 1, 2 * half), cmp_w)


def _nsa_body(qr_ref, qn_ref, kc_ref, vc_ref, kv_ref, sm_ref, e_ref, o_ref, *, tq):
    g = pl.program_id(1)
    qi = pl.program_id(2)
    lane256 = lax.broadcasted_iota(jnp.int32, (1, 2 * LANES), 1)
    lane128 = lax.broadcasted_iota(jnp.int32, (1, LANES), 1)
    lane64 = lax.broadcasted_iota(jnp.int32, (1, NSA_SEL_LEN), 1)
    row = lax.broadcasted_iota(jnp.int32, (tq, 1), 0)
    col = lax.broadcasted_iota(jnp.int32, (1, tq), 1)
    qpos = qi * tq + row
    head_lanes = [(lane256 >= NSA_DH * j) & (lane256 < NSA_DH * (j + 1)) for j in range(NSA_HPG)]
    nt_dims = (((1,), (1,)), ((), ()))

    blk = _NSA_R * (lane256 % NSA_SEL_LEN) + lane256 // NSA_SEL_LEN
    cmask = blk * NSA_CMP_STRIDE + (NSA_CMP_LEN - 1) <= qpos
    qn = qn_ref[...] * (NSA_DH ** -0.5)
    imp = jnp.zeros((tq, _NSA_NC), F32)
    o_cmp = []
    for j in range(NSA_HPG):
        qj = jnp.where(head_lanes[j], qn, 0.0).astype(BF16)
        s = lax.dot_general(qj, kc_ref[...], nt_dims, preferred_element_type=F32)
        s = jnp.where(cmask, s, -1e30)
        e = jnp.where(cmask, jnp.exp(s - jnp.max(s, axis=1, keepdims=True)), 0.0)
        p = e / jnp.maximum(jnp.sum(e, axis=1, keepdims=True), 1e-30)
        imp = imp + p
        o_cmp.append(jnp.dot(p.astype(BF16), vc_ref[...], preferred_element_type=F32))
    imps = (imp[:, 0:64] + imp[:, 64:128]) + (imp[:, 128:192] + imp[:, 192:256])

    qb = qpos // NSA_SEL_LEN
    back = qb - lane64
    score = jnp.where(back == 0, FORCE_SCORE, imps)
    score = jnp.where(back == 1, FORCE_SCORE, score)
    score = jnp.where(back == qb, FORCE_SCORE, score)
    score = jnp.where(back >= 0, score, -FORCE_SCORE)
    rank = jnp.zeros((tq, NSA_SEL_LEN), F32)
    for i in range(NSA_SEL_LEN):
        ci = score[:, i:i + 1]
        ge = jnp.where(ci >= score, 1.0, 0.0)
        gt = jnp.where(ci > score, 1.0, 0.0)
        later = jnp.where(lane64 > i, 1.0, 0.0)
        rank = rank + (gt + later * (ge - gt))
    sel = jnp.where(rank < NSA_TOPK, 1.0, 0.0) * jnp.where(back >= 0, 1.0, 0.0)
    sel = sel.astype(BF16)

    qr = qr_ref[...] * (NSA_DH ** -0.5)
    qjs = [jnp.where(head_lanes[j], qr, 0.0).astype(BF16) for j in range(NSA_HPG)]
    rows4 = NSA_HPG * tq

    def attend(t, carry, kcol, bias):
        m, l, acc = carry
        start = pl.multiple_of(t * tq, tq)
        k = kv_ref[pl.ds(start, tq), kcol:kcol + 2 * LANES]
        v = kv_ref[pl.ds(start, tq), 4 * LANES:5 * LANES]
        s = jnp.concatenate([lax.dot_general(qj, k, nt_dims, preferred_element_type=F32) + bias for qj in qjs],
                            axis=0)
        m_new = jnp.maximum(m, jnp.max(s, axis=1, keepdims=True))
        alpha = jnp.exp(m - m_new)
        p = jnp.exp(s - m_new)
        l = alpha * l + jnp.sum(p, axis=1, keepdims=True)
        acc = alpha * acc + jnp.dot(p.astype(BF16), v, preferred_element_type=F32)
        return m_new, l, acc

    def sel_bias(t, extra=None):
        on = jnp.dot(sel, e_ref[t], preferred_element_type=F32) > 0.5
        inner = 0.0 if extra is None else jnp.where(extra, 0.0, -1e30)
        return jnp.where(on, inner, -1e30)

    init = (jnp.full((rows4, 1), -1e30, F32), jnp.zeros((rows4, 1), F32), jnp.zeros((rows4, LANES), F32))
    causal = col <= row
    carry = lax.fori_loop(0, qi, lambda t, c: attend(t, c, 0, sel_bias(t)), init)
    _, l_s, acc_s = attend(qi, carry, 0, sel_bias(qi, causal))

    neg = jnp.full((tq, tq), -1e30, F32)
    carry = attend(qi, init, 2 * LANES, jnp.where(causal, 0.0, -1e30))
    carry = attend(jnp.maximum(qi - 1, 0), carry, 2 * LANES, jnp.where(qi >= 1, jnp.zeros((tq, tq), F32), neg))
    _, l_w, acc_w = attend(jnp.maximum(qi - 2, 0), carry, 2 * LANES,
                           jnp.where(qi >= 2, jnp.where(col > row, 0.0, -1e30), neg))

    sm = sm_ref[...]
    ys = []
    for j in range(NSA_HPG):
        base = 8 + 3 * (NSA_HPG * g + j)
        gate = [jax.nn.sigmoid(jnp.sum(jnp.where(lane128 == base + c, sm, 0.0), axis=1, keepdims=True))
                for c in range(3)]
        r = slice(j * tq, (j + 1) * tq)
        comb = jnp.where(lane128 < NSA_DH, acc_s[r] / l_s[r] * gate[1], acc_w[r] / l_w[r] * gate[2])
        comb = comb + o_cmp[j] * gate[0]
        ys.append(comb + pltpu.roll(comb, NSA_DH, 1))
    o_ref[:, 0:LANES] = jnp.where(lane128 < NSA_DH, ys[0], ys[1])
    o_ref[:, LANES:2 * LANES] = jnp.where(lane128 < NSA_DH, ys[2], ys[3])


def _nsa_prompt(oq, onsa_bf, osm, kc, B, S, tq=256):
    nq = S // tq
    nsb = S // NSA_SEL_LEN
    perm = jnp.transpose(kc.reshape(B, 4, nsb, _NSA_R, LANES), (0, 1, 3, 2, 4)).reshape(B, 4, _NSA_NC, LANES)
    kc_rep = jnp.tile(perm[:, 0:2, :, 0:NSA_DH], (1, 1, 1, NSA_HPG)).astype(BF16)
    vc = perm[:, 2:4].astype(BF16)
    n_kt = S // tq
    e = (jnp.arange(nsb)[None, :, None]
         == (jnp.arange(n_kt)[:, None, None] * (tq // NSA_SEL_LEN) + jnp.arange(tq)[None, None, :] // NSA_SEL_LEN))
    e = e.astype(BF16)
    return pl.pallas_call(
        functools.partial(_nsa_body, tq=tq),
        grid=(B, NSA_GROUPS, nq),
        in_specs=[pl.BlockSpec((tq, 256), lambda b, g, i: (b * nq + i, 2 + g)),
                  pl.BlockSpec((tq, 256), lambda b, g, i: (b * nq + i, 4 + g)),
                  pl.BlockSpec((None, None, _NSA_NC, 256), lambda b, g, i: (b, g, 0, 0)),
                  pl.BlockSpec((None, None, _NSA_NC, LANES), lambda b, g, i: (b, g, 0, 0)),
                  pl.BlockSpec((S, _NSA_BF_W), lambda b, g, i: (b, g)),
                  pl.BlockSpec((tq, LANES), lambda b, g, i: (b * nq + i, 0)),
                  pl.BlockSpec((n_kt, nsb, tq), lambda b, g, i: (0, 0, 0))],
        out_specs=pl.BlockSpec((tq, 256), lambda b, g, i: (b * nq + i, g)),
        out_shape=jax.ShapeDtypeStruct((B * S, 512), F32),
        compiler_params=pltpu.CompilerParams(vmem_limit_bytes=VMEM_LIMIT),
        name="nsa_attn",
    )(oq, oq, kc_rep, vc, onsa_bf, osm, e)


def _merge_body(x_ref, ya_ref, yb_ref, yc_ref, gm_ref, sc1_ref, sh1_ref, g1_ref, gf_ref, sc2_ref, sh2_ref,
                wbg_ref, bbg_ref, wbr_ref, wout_ref, wrh_ref, wrl_ref, br_ref, xo_ref, h2_ref, lg_ref):
    x = x_ref[...]
    hb = _norm_mod(x, gm_ref[...], sc1_ref[...], sh1_ref[...]).astype(BF16)
    mix = None
    for n, y_ref in enumerate((ya_ref, yb_ref, yc_ref)):
        gate = jax.nn.sigmoid(jnp.dot(hb, wbg_ref[:, n * D_MODEL:(n + 1) * D_MODEL], preferred_element_type=F32)
                              + bbg_ref[:, n * D_MODEL:(n + 1) * D_MODEL])
        proj = jnp.dot(y_ref[...].astype(BF16), wbr_ref[n], preferred_element_type=F32)
        mix = gate * proj if mix is None else mix + gate * proj
    y = jnp.dot(mix.astype(BF16), wout_ref[...], preferred_element_type=F32)
    xn = x + g1_ref[...] * y
    xo_ref[...] = xn
    h2 = _norm_mod(xn, gf_ref[...], sc2_ref[...], sh2_ref[...])
    hi = h2.astype(BF16)
    lo = (h2 - hi.astype(F32)).astype(BF16)
    h2_ref[...] = hi
    lg_ref[...] = (jnp.dot(hi, wrh_ref[...], preferred_element_type=F32)
                   + jnp.dot(lo, wrh_ref[...], preferred_element_type=F32)
                   + jnp.dot(hi, wrl_ref[...], preferred_element_type=F32) + br_ref[...])


def _merge(x, ya, yb, yc, gm, mods, gf, wbg, bbg, wbr, wout, wrh, wrl, br, tm, rows_per_seq):
    T = x.shape[0]
    nt = T // tm
    tiles_per_seq = max(rows_per_seq // tm, 1)
    sc1, sh1, g1, sc2, sh2 = mods
    R = sc1.shape[1]
    row = lambda w: pl.BlockSpec((tm, w), lambda i: (i, 0))
    mod_spec = pl.BlockSpec((None, R, D_MODEL), lambda i: (i // tiles_per_seq, 0, 0))
    const = lambda shape: pl.BlockSpec(shape, lambda i: (0,) * len(shape))
    return pl.pallas_call(
        _merge_body,
        grid=(nt,),
        in_specs=[row(D_MODEL), row(512), row(512), row(512), const((1, D_MODEL)), mod_spec, mod_spec, mod_spec,
                  const((1, D_MODEL)), mod_spec, mod_spec,
                  const((D_MODEL, 3 * D_MODEL)), const((1, 3 * D_MODEL)), const((3, BRANCH_W, D_MODEL)),
                  const((D_MODEL, D_MODEL)), const((D_MODEL, LANES)), const((D_MODEL, LANES)), const((1, LANES))],
        out_specs=[row(D_MODEL), row(D_MODEL), row(LANES)],
        out_shape=[jax.ShapeDtypeStruct((T, D_MODEL), F32), jax.ShapeDtypeStruct((T, D_MODEL), BF16),
                   jax.ShapeDtypeStruct((T, LANES), F32)],
        compiler_params=pltpu.CompilerParams(vmem_limit_bytes=VMEM_LIMIT),
        name="merge",
    )(x, ya, yb, yc, gm.reshape(1, D_MODEL), sc1, sh1, g1, gf.reshape(1, D_MODEL), sc2, sh2,
      wbg, bbg.reshape(1, 3 * D_MODEL), wbr, wout, wrh, wrl, br)


def _moe_body(te_ref, tv_ref, x_ref, wu_ref, bu_ref, wd_ref, bd_ref, o_ref, wu_s, wd_s):
    j = pl.program_id(0)
    e = te_ref[j]
    e_prev = te_ref[jnp.maximum(j - 1, 0)]

    @pl.when((j == 0) | (e != e_prev))
    def _():
        wu_s[...] = wu_ref[0].astype(BF16)
        wd_s[...] = wd_ref[0].astype(BF16)

    @pl.when(tv_ref[j] == 1)
    def _():
        uu = jnp.dot(x_ref[...], wu_s[...], preferred_element_type=F32) + bu_ref[0]
        gl = jnp.minimum(uu[:, :D_FF], SWIGLU_LIMIT)
        up = jnp.clip(uu[:, D_FF:], -SWIGLU_LIMIT, SWIGLU_LIMIT)
        act = gl * jax.nn.sigmoid(SWIGLU_ALPHA * gl) * (up + 1.0)
        o_ref[...] = jnp.dot(act.astype(BF16), wd_s[...], preferred_element_type=F32) + bd_ref[0]

    @pl.when(tv_ref[j] == 0)
    def _():
        o_ref[...] = jnp.zeros_like(o_ref)


def _moe_grouped(x_pad, tile_e, tile_valid, w_up, b_up, w_down, b_down, tm):
    n_tiles = x_pad.shape[0] // tm
    grid_spec = pltpu.PrefetchScalarGridSpec(
        num_scalar_prefetch=2,
        grid=(n_tiles,),
        in_specs=[pl.BlockSpec((tm, D_MODEL), lambda j, te, tv: (j, 0)),
                  pl.BlockSpec((1, D_MODEL, 2 * D_FF), lambda j, te, tv: (te[j], 0, 0)),
                  pl.BlockSpec((1, 1, 2 * D_FF), lambda j, te, tv: (te[j], 0, 0)),
                  pl.BlockSpec((1, D_FF, D_MODEL), lambda j, te, tv: (te[j], 0, 0)),
                  pl.BlockSpec((1, 1, D_MODEL), lambda j, te, tv: (te[j], 0, 0))],
        out_specs=pl.BlockSpec((tm, D_MODEL), lambda j, te, tv: (j, 0)),
        scratch_shapes=[pltpu.VMEM((D_MODEL, 2 * D_FF), BF16), pltpu.VMEM((D_FF, D_MODEL), BF16)],
    )
    return pl.pallas_call(
        _moe_body,
        grid_spec=grid_spec,
        out_shape=jax.ShapeDtypeStruct((x_pad.shape[0], D_MODEL), F32),
        compiler_params=pltpu.CompilerParams(vmem_limit_bytes=VMEM_LIMIT, dimension_semantics=("arbitrary",)),
        name="moe",
    )(tile_e, tile_valid, x_pad, w_up, b_up.reshape(N_EXPERTS, 1, 2 * D_FF), w_down,
      b_down.reshape(N_EXPERTS, 1, D_MODEL))


def _moe(h2, logits, w_up, b_up, w_down, b_down, tm=256):
    T = h2.shape[0]
    A = T * TOP_K
    top_v, top_i = lax.top_k(logits, TOP_K)
    wts = jax.nn.softmax(top_v, axis=-1)
    flat_e = top_i.reshape(A).astype(jnp.int32)
    order = jnp.argsort(flat_e, stable=True).astype(jnp.int32)
    sorted_e = flat_e[order]
    counts = jnp.zeros((N_EXPERTS,), jnp.int32).at[flat_e].add(1)
    tiles_per_e = (counts + tm - 1) // tm
    tile_end = jnp.cumsum(tiles_per_e)
    pad_start = (tile_end - tiles_per_e) * tm
    grp_start = jnp.cumsum(counts) - counts
    pos_sorted = pad_start[sorted_e] + (jnp.arange(A, dtype=jnp.int32) - grp_start[sorted_e])
    n_tiles = (A + N_EXPERTS * (tm - 1)) // tm + 1
    NP = n_tiles * tm
    src_tok = jnp.full((NP,), T, jnp.int32).at[pos_sorted].set(order // TOP_K)
    x_ext = jnp.concatenate([h2, jnp.zeros((1, D_MODEL), h2.dtype)], axis=0)
    x_pad = x_ext[src_tok]
    tile_idx = jnp.arange(n_tiles, dtype=jnp.int32)
    tile_e = jnp.minimum(jnp.searchsorted(tile_end, tile_idx, side='right'), N_EXPERTS - 1).astype(jnp.int32)
    tile_valid = (tile_idx < tile_end[-1]).astype(jnp.int32)
    last_e = tile_e[jnp.maximum(tile_end[-1] - 1, 0)]
    tile_e = jnp.where(tile_valid == 1, tile_e, last_e)
    y_pad = _moe_grouped(x_pad, tile_e, tile_valid, w_up, b_up, w_down, b_down, tm)
    inv_pos = jnp.zeros((A,), jnp.int32).at[order].set(pos_sorted)
    y_sel = y_pad[inv_pos].reshape(T, TOP_K, D_MODEL)
    return jnp.sum(wts[:, :, None] * y_sel, axis=1)


def _dot_hp(a, b):
    return jnp.dot(a, b, precision=lax.Precision.HIGHEST, preferred_element_type=F32)


_IN_ROPE_FLAGS = np.concatenate([np.full((w // LANES,), int(r), np.int32) for (_, w, r, _, _) in _IN_PLAN])


def _in_s_body(flag_ref, x_ref, g_ref, sc_ref, sh_ref, cos_ref, sin_ref, w_ref, o_ref):
    n = pl.program_id(0)
    h = _norm_mod(x_ref[...], g_ref[...], sc_ref[...], sh_ref[...])
    y = _dot_hp(h, w_ref[...])
    lane = lax.broadcasted_iota(jnp.int32, (1, LANES), 1)
    yr = _rope128(y, cos_ref[...], sin_ref[...], (lane % DA_QK) < (DA_QK // 2))
    o_ref[...] = jnp.where(flag_ref[n] == 1, yr, y)


def _in_proj_sample(x, g, sc, sh, cos_t, sin_t, w_packed):
    T = x.shape[0]
    full = lambda shape: pl.BlockSpec(shape, lambda n, f: (0,) * len(shape))
    grid_spec = pltpu.PrefetchScalarGridSpec(
        num_scalar_prefetch=1,
        grid=(_W_IN_PACKED // LANES,),
        in_specs=[full((T, D_MODEL)), full((1, D_MODEL)), full((T, D_MODEL)), full((T, D_MODEL)),
                  full((T, LANES)), full((T, LANES)),
                  pl.BlockSpec((D_MODEL, LANES), lambda n, f: (0, n))],
        out_specs=pl.BlockSpec((T, LANES), lambda n, f: (0, n)),
    )
    return pl.pallas_call(
        _in_s_body, grid_spec=grid_spec,
        out_shape=jax.ShapeDtypeStruct((T, _W_IN_PACKED), F32),
        name="in_proj_sample",
    )(jnp.asarray(_IN_ROPE_FLAGS), x, g.reshape(1, D_MODEL), sc, sh, cos_t, sin_t, w_packed)


def _mix_s_body(x_ref, g_ref, sc_ref, sh_ref, ya_ref, yb_ref, yc_ref, wg0, wg1, wg2, bg0, bg1, bg2, wbr_ref, o_ref):
    h = _norm_mod(x_ref[...], g_ref[...], sc_ref[...], sh_ref[...])
    mix = None
    for n, (y_ref, wg, bg) in enumerate(((ya_ref, wg0, bg0), (yb_ref, wg1, bg1), (yc_ref, wg2, bg2))):
        gate = jax.nn.sigmoid(_dot_hp(h, wg[...]) + bg[...])
        proj = _dot_hp(y_ref[...], wbr_ref[n])
        mix = gate * proj if mix is None else mix + gate * proj
    o_ref[...] = mix


def _gate_mix_sample(x, g, sc, sh, ya, yb, yc, w_bgate, b_bgate, w_branch, tn=256):
    T = x.shape[0]
    nj = D_MODEL // tn
    full = lambda shape: pl.BlockSpec(shape, lambda j: (0,) * len(shape))
    wg = [pl.BlockSpec((D_MODEL, tn), lambda j, n=n: (0, n * nj + j)) for n in range(N_BRANCH)]
    bg = [pl.BlockSpec((1, tn), lambda j, n=n: (0, n * nj + j)) for n in range(N_BRANCH)]
    bb = b_bgate.reshape(1, N_BRANCH * D_MODEL)
    return pl.pallas_call(
        _mix_s_body, grid=(nj,),
        in_specs=[full((T, D_MODEL)), full((1, D_MODEL)), full((T, D_MODEL)), full((T, D_MODEL)),
                  full((T, BRANCH_W)), full((T, BRANCH_W)), full((T, BRANCH_W))] + wg + bg
                 + [pl.BlockSpec((N_BRANCH, BRANCH_W, tn), lambda j: (0, 0, j))],
        out_specs=pl.BlockSpec((T, tn), lambda j: (0, j)),
        out_shape=jax.ShapeDtypeStruct((T, D_MODEL), F32),
        name="gate_mix_sample",
    )(x, g.reshape(1, D_MODEL), sc, sh, ya, yb, yc, w_bgate, w_bgate, w_bgate, bb, bb, bb, w_branch)


def _out_s_body(x_ref, mix_ref, g1_ref, wout_ref, gf_ref, sc2_ref, sh2_ref, wr_ref, br_ref, xo_ref, h2_ref, lg_ref):
    xn = x_ref[...] + g1_ref[...] * _dot_hp(mix_ref[...], wout_ref[...])
    xo_ref[...] = xn
    h2 = _norm_mod(xn, gf_ref[...], sc2_ref[...], sh2_ref[...])
    h2_ref[...] = h2
    lg_ref[...] = _dot_hp(h2, wr_ref[...]) + br_ref[...]


def _out_sample(x, mix, g1, w_out, gf, sc2, sh2, wr, br):
    T = x.shape[0]
    return pl.pallas_call(
        _out_s_body,
        out_shape=[jax.ShapeDtypeStruct((T, D_MODEL), F32), jax.ShapeDtypeStruct((T, D_MODEL), F32),
                   jax.ShapeDtypeStruct((T, LANES), F32)],
        compiler_params=pltpu.CompilerParams(vmem_limit_bytes=VMEM_LIMIT),
        name="out_sample",
    )(x, mix, g1, w_out, gf.reshape(1, D_MODEL), sc2, sh2, wr, br)


def _moe_s_body(h_ref, gate_ref, wu_ref, bu_ref, wd_ref, bd_ref, x_ref, g2_ref, o_ref, acc):
    e = pl.program_id(0)

    @pl.when(e == 0)
    def _():
        acc[...] = jnp.zeros_like(acc)

    uu = _dot_hp(h_ref[...], wu_ref[0]) + bu_ref[0]
    gl = jnp.minimum(uu[:, :D_FF], SWIGLU_LIMIT)
    up = jnp.clip(uu[:, D_FF:], -SWIGLU_LIMIT, SWIGLU_LIMIT)
    act = gl * jax.nn.sigmoid(SWIGLU_ALPHA * gl) * (up + 1.0)
    acc[...] += gate_ref[0] * (_dot_hp(act, wd_ref[0]) + bd_ref[0])

    @pl.when(e == N_EXPERTS - 1)
    def _():
        o_ref[...] = x_ref[...] + g2_ref[...] * acc[...]


def _moe_sample(h2, logits, w_up, b_up, w_down, b_down, x, g2):
    T = h2.shape[0]
    top_v, top_i = lax.top_k(logits, TOP_K)
    wts = jax.nn.softmax(top_v, axis=-1)
    gate = jnp.einsum('tk,tke->et', wts, jax.nn.one_hot(top_i, N_EXPERTS, dtype=F32),
                      precision=lax.Precision.HIGHEST)[:, :, None]
    full = lambda shape: pl.BlockSpec(shape, lambda e: (0,) * len(shape))
    return pl.pallas_call(
        _moe_s_body, grid=(N_EXPERTS,),
        in_specs=[full((T, D_MODEL)), pl.BlockSpec((1, T, 1), lambda e: (e, 0, 0)),
                  pl.BlockSpec((1, D_MODEL, 2 * D_FF), lambda e: (e, 0, 0)),
                  pl.BlockSpec((1, 1, 2 * D_FF), lambda e: (e, 0, 0)),
                  pl.BlockSpec((1, D_FF, D_MODEL), lambda e: (e, 0, 0)),
                  pl.BlockSpec((1, 1, D_MODEL), lambda e: (e, 0, 0)),
                  full((T, D_MODEL)), full((T, D_MODEL))],
        out_specs=full((T, D_MODEL)),
        out_shape=jax.ShapeDtypeStruct((T, D_MODEL), F32),
        scratch_shapes=[pltpu.VMEM((T, D_MODEL), F32)],
        compiler_params=pltpu.CompilerParams(vmem_limit_bytes=VMEM_LIMIT, dimension_semantics=("arbitrary",)),
        name="moe_sample",
    )(h2, gate, w_up, b_up.reshape(N_EXPERTS, 1, 2 * D_FF), w_down, b_down.reshape(N_EXPERTS, 1, D_MODEL), x, g2)


def _final_norm_body(x_ref, g_ref, o_ref):
    x = x_ref[...]
    o_ref[...] = x * lax.rsqrt(jnp.mean(x * x, axis=-1, keepdims=True) + EPS) * g_ref[...]


def _final_norm(x, g, tm):
    T = x.shape[0]
    return pl.pallas_call(
        _final_norm_body,
        grid=(T // tm,),
        in_specs=[pl.BlockSpec((tm, D_MODEL), lambda i: (i, 0)), pl.BlockSpec((1, D_MODEL), lambda i: (0, 0))],
        out_specs=pl.BlockSpec((tm, D_MODEL), lambda i: (i, 0)),
        out_shape=jax.ShapeDtypeStruct((T, D_MODEL), F32),
        name="final_norm",
    )(x, g.reshape(1, D_MODEL))


def _rms_norm(x, g):
    y = x * lax.rsqrt(jnp.mean(x * x, axis=-1, keepdims=True) + EPS)
    return y * g


def _masked_softmax(s, mask):
    s = jnp.where(mask, s, -1e30)
    m = jnp.max(s, axis=-1, keepdims=True)
    e = jnp.where(mask, jnp.exp(s - m), 0.0)
    return e / jnp.maximum(jnp.sum(e, axis=-1, keepdims=True), 1e-30)


def _sweep_queries(fn, qs, qpos, block):
    B, Q = qs[0].shape[:2]
    nb = Q // block

    def split(a):
        return jnp.moveaxis(a.reshape((B, nb, block) + a.shape[2:]), 1, 0)

    out = lax.map(lambda xs: fn(*xs[0], xs[1]), (tuple(split(a) for a in qs), qpos.reshape(nb, block)))
    out = jnp.moveaxis(out, 0, 1)
    return out.reshape((B, Q) + out.shape[3:])


def _diff_attn_core(q, k, v, qpos, kpos, lam, lam_init, g_sub):
    s = jnp.einsum('bqhcd,bkhcd->bhcqk', q, k) * (DA_QK ** -0.5)
    mask = kpos[None, :] <= qpos[:, None]
    p = _masked_softmax(s, mask)
    a = p[:, :, 0] - lam * p[:, :, 1]
    o = jnp.einsum('bhqk,bkhd->bqhd', a, v)
    return _rms_norm(o, g_sub) * (1.0 - lam_init)


def _mlstm_chunked(q, k, v, log_i, log_f, C0, n0, m0):
    B, S, H, DK = q.shape
    L = math.gcd(S, ML_CHUNK)
    nc = S // L

    def to_chunks(a):
        return jnp.moveaxis(a.reshape((B, nc, L) + a.shape[2:]), 1, 0)

    kf = k * (DK ** -0.5)
    causal = jnp.tril(jnp.ones((L, L), dtype=bool))[None, :, :, None]

    def step(carry, xs):
        C, n, m = carry
        qc, kc, vc, ic, fc = xs
        b = jnp.cumsum(fc, axis=1)
        a = b + m[:, None, :]
        D = jnp.where(causal, b[:, :, None, :] - b[:, None, :, :] + ic[:, None, :, :], -jnp.inf)
        mt = jnp.maximum(a, jnp.max(D, axis=2))
        w_inter = jnp.exp(a - mt)
        w_intra = jnp.exp(D - mt[:, :, None, :])
        sc = w_intra * jnp.einsum('bthd,bshd->btsh', qc, kc)
        num = w_inter[..., None] * jnp.einsum('bhvd,bthd->bthv', C, qc) + jnp.einsum('btsh,bshv->bthv', sc, vc)
        den = w_inter * jnp.einsum('bhd,bthd->bth', n, qc) + jnp.sum(sc, axis=2)
        h = num / jnp.maximum(jnp.abs(den), jnp.exp(-mt))[..., None]
        m_new = mt[:, -1]
        wi = w_intra[:, -1]
        wc = w_inter[:, -1]
        C_new = wc[..., None, None] * C + jnp.einsum('bsh,bshv,bshd->bhvd', wi, vc, kc)
        n_new = wc[..., None] * n + jnp.einsum('bsh,bshd->bhd', wi, kc)
        return (C_new, n_new, m_new), h

    (C1, n1, m1), hs = lax.scan(step, (C0, n0, m0),
                                (to_chunks(q), to_chunks(kf), to_chunks(v), to_chunks(log_i), to_chunks(log_f)))
    hs = jnp.moveaxis(hs, 0, 1).reshape(B, S, H, v.shape[-1])
    return hs, C1, n1, m1


def _nsa_compress(rows, pos_emb, w):
    B, T, G, dh = rows.shape
    Tp = -(-T // NSA_SEL_LEN) * NSA_SEL_LEN
    rows = jnp.pad(rows, ((0, 0), (0, Tp - T), (0, 0), (0, 0)))
    ch = rows.reshape(B, Tp // NSA_CMP_STRIDE, NSA_CMP_STRIDE, G, dh)
    n_sub = NSA_CMP_LEN // NSA_CMP_STRIDE
    nc = ch.shape[1] - n_sub + 1
    blocks = jnp.concatenate([ch[:, i:i + nc] for i in range(n_sub)], axis=2)
    blocks = blocks + pos_emb[None, None, :, None, :]
    return jnp.einsum('bnlgd,lde->bnge', blocks, w.reshape(NSA_CMP_LEN, dh, dh))


def _nsa_cmp_branch(q, kc, vc, qpos):
    B, Q, H, dh = q.shape
    nc = kc.shape[1]
    qg = q.reshape(B, Q, NSA_GROUPS, NSA_HPG, dh)
    s = jnp.einsum('bqgjd,bngd->bgjqn', qg, kc) * (dh ** -0.5)
    ends = jnp.arange(nc) * NSA_CMP_STRIDE + NSA_CMP_LEN - 1
    mask = ends[None, :] <= qpos[:, None]
    p = _masked_softmax(s, mask)
    o = jnp.einsum('bgjqn,bngd->bqgjd', p, vc).reshape(B, Q, H, dh)
    return o, jnp.sum(p, axis=2)


def _nsa_select(imp_cmp, qpos, nsb):
    r = NSA_SEL_LEN // NSA_CMP_STRIDE
    imp = jnp.pad(imp_cmp, ((0, 0), (0, 0), (0, 0), (0, nsb * r - imp_cmp.shape[-1])))
    imp = jnp.sum(imp.reshape(imp.shape[:3] + (nsb, r)), axis=-1)
    j = jnp.arange(nsb)[None, :]
    qb = (qpos // NSA_SEL_LEN)[:, None]
    forced = (j == 0) | (j == qb) | (j == qb - 1)
    score = jnp.where(forced, FORCE_SCORE, imp)
    score = jnp.where(j <= qb, score, -FORCE_SCORE)
    _, idx = lax.top_k(score, min(NSA_TOPK, nsb))
    valid = idx <= (qpos // NSA_SEL_LEN)[None, None, :, None]
    return jnp.transpose(idx, (0, 2, 1, 3)), jnp.transpose(valid, (0, 2, 1, 3))


def _to_sel_blocks(rows, nsb):
    B, T, G, dh = rows.shape
    rows = jnp.pad(rows, ((0, 0), (0, nsb * NSA_SEL_LEN - T), (0, 0), (0, 0)))
    return jnp.transpose(rows.reshape(B, nsb, NSA_SEL_LEN, G, dh), (0, 3, 1, 2, 4))


def _nsa_slc_branch(q, idx, valid, qpos, kblk, vblk):
    B, Qb, H, dh = q.shape
    bi = jnp.arange(B)[:, None, None, None]
    gi = jnp.arange(NSA_GROUPS)[None, None, :, None]
    ks = kblk[bi, gi, idx]
    vs = vblk[bi, gi, idx]
    kpos = idx[..., None] * NSA_SEL_LEN + jnp.arange(NSA_SEL_LEN)
    mask = valid[..., None] & (kpos <= qpos[None, :, None, None, None])
    qg = q.reshape(B, Qb, NSA_GROUPS, NSA_HPG, dh)
    s = jnp.einsum('bqgjd,bqgnld->bqgjnl', qg, ks) * (dh ** -0.5)
    p = _masked_softmax(s.reshape(B, Qb, NSA_GROUPS, NSA_HPG, -1), mask.reshape(B, Qb, NSA_GROUPS, 1, -1))
    o = jnp.einsum('bqgjm,bqgmd->bqgjd', p, vs.reshape(B, Qb, NSA_GROUPS, -1, dh))
    return o.reshape(B, Qb, H, dh)


def _nsa_win_branch(q, kw, vw, qpos, kpos):
    B, NB, QB, H, dh = q.shape
    qg = q.reshape(B, NB, QB, NSA_GROUPS, NSA_HPG, dh)
    s = jnp.einsum('bnqgjd,bnkgd->bngjqk', qg, kw) * (dh ** -0.5)
    dpos = qpos[:, :, None] - kpos[:, None, :]
    mask = (dpos >= 0) & (dpos < NSA_WINDOW) & (kpos[:, None, :] >= 0)
    p = _masked_softmax(s, mask[None, :, None, None])
    o = jnp.einsum('bngjqk,bnkgd->bnqgjd', p, vw)
    return o.reshape(B, NB * QB, H, dh)


def _gather_pages(cache, l, page_table):
    g = cache[l, page_table]
    return g.reshape((g.shape[0], g.shape[1] * g.shape[2]) + g.shape[3:])


def _mlstm_mixer(oml, osm, gate_b, norm_g, B, Q, past):
    qm = oml[:, 0:256].reshape(B, Q, ML_HEADS, ML_QK)
    km = oml[:, 256:512].reshape(B, Q, ML_HEADS, ML_QK)
    vm = oml[:, 512:1024].reshape(B, Q, ML_HEADS, ML_V)
    om = oml[:, 1024:1536].reshape(B, Q, ML_HEADS, ML_V)
    im = osm[:, 0:4].reshape(B, Q, ML_HEADS)
    fm = osm[:, 4:8].reshape(B, Q, ML_HEADS)
    log_i = im + gate_b[:ML_HEADS]
    log_f = jax.nn.log_sigmoid(fm + gate_b[ML_HEADS:])
    if past is None:
        C0 = jnp.zeros((B, ML_HEADS, ML_V, ML_QK), F32)
        n0 = jnp.zeros((B, ML_HEADS, ML_QK), F32)
        m0 = jnp.zeros((B, ML_HEADS), F32)
    else:
        C0, n0, m0 = past
    hm, C1, n1, m1 = _mlstm_chunked(qm, km, vm, log_i, log_f, C0, n0, m0)
    hm = _rms_norm(hm, norm_g) * jax.nn.sigmoid(om)
    return hm.reshape(B * Q, ML_HEADS * ML_V), C1, n1, m1


def _nsa_mixer(oq, onkv, owin, osm, pos, cmp_pos, cmp_w, B, Q, past):
    qn_rot = oq[:, 512:1024].reshape(B, Q, NSA_HEADS, NSA_DH)
    qn = oq[:, 1024:1536].reshape(B, Q, NSA_HEADS, NSA_DH)
    nk = onkv.reshape(B, Q, 4, NSA_GROUPS, NSA_DH)
    cmp_k, cmp_v, slc_k, slc_v = nk[:, :, 0], nk[:, :, 1], nk[:, :, 2], nk[:, :, 3]
    wk = owin.reshape(B, Q, 2, NSA_GROUPS, NSA_DH)
    win_k, win_v = wk[:, :, 0], wk[:, :, 1]
    if past is not None:
        pn, buf = past
        cmp_k = jnp.concatenate([pn[:, :, 0], cmp_k], axis=1)
        cmp_v = jnp.concatenate([pn[:, :, 1], cmp_v], axis=1)
        slc_k = jnp.concatenate([pn[:, :, 2], slc_k], axis=1)
        slc_v = jnp.concatenate([pn[:, :, 3], slc_v], axis=1)
    nsb = -(-cmp_k.shape[1] // NSA_SEL_LEN)
    kc = _nsa_compress(cmp_k, cmp_pos[0], cmp_w[0])
    vc = _nsa_compress(cmp_v, cmp_pos[1], cmp_w[1])
    o_cmp, imp = _nsa_cmp_branch(qn, kc, vc, pos)
    idx, valid = _nsa_select(imp, pos, nsb)
    kblk = _to_sel_blocks(slc_k, nsb)
    vblk = _to_sel_blocks(slc_v, nsb)
    o_slc = _sweep_queries(lambda q, ix, ok, p: _nsa_slc_branch(q, ix, ok, p, kblk, vblk),
                           (qn_rot, idx, valid), pos, math.gcd(Q, NSA_QBLOCK))
    if past is None:
        win_len = min(NSA_WINDOW, Q)
        QB = math.gcd(Q, NSA_QBLOCK)
        nb, nw = Q // QB, NSA_WINDOW // QB
        kp = jnp.pad(win_k, ((0, 0), (NSA_WINDOW, 0), (0, 0), (0, 0))).reshape(B, nb + nw, QB, NSA_GROUPS, NSA_DH)
        vp = jnp.pad(win_v, ((0, 0), (NSA_WINDOW, 0), (0, 0), (0, 0))).reshape(B, nb + nw, QB, NSA_GROUPS, NSA_DH)
        band_k = jnp.concatenate([kp[:, i:i + nb] for i in range(nw + 1)], axis=2)
        band_v = jnp.concatenate([vp[:, i:i + nb] for i in range(nw + 1)], axis=2)
        kpos_w = (jnp.arange(nb) * QB)[:, None] - NSA_WINDOW + jnp.arange((nw + 1) * QB)[None, :]
        o_win = _nsa_win_branch(qn_rot.reshape(B, nb, QB, NSA_HEADS, NSA_DH), band_k, band_v,
                                pos.reshape(nb, QB), kpos_w)
        new_win = jnp.stack([win_k, win_v], axis=2)[:, Q - win_len:]
    else:
        win_len = buf.shape[1]
        kw = jnp.concatenate([buf[:, :, 0], win_k], axis=1)
        vw = jnp.concatenate([buf[:, :, 1], win_v], axis=1)
        kpos_w = (pos[0] - win_len) + jnp.arange(win_len + Q)
        o_win = _nsa_win_branch(qn_rot[:, None], kw[:, None], vw[:, None], pos[None, :], kpos_w[None, :])
        new_win = jnp.stack([kw, vw], axis=2)[:, -win_len:]
    gn = jax.nn.sigmoid(osm[:, 8:32].reshape(B, Q, NSA_HEADS, 3))
    yc = (gn[..., 0:1] * o_cmp + gn[..., 1:2] * o_slc + gn[..., 2:3] * o_win).reshape(B * Q, NSA_HEADS * NSA_DH)
    return yc, new_win


def _diff_attn_sample(oq, odkv, pd, pos, da_lam, g_sub, lam_init, B, Q):
    qa = oq[:, 0:512].reshape(B, Q, DA_HEADS, 2, DA_QK)
    ka = odkv[:, 0:512].reshape(B, Q, DA_HEADS, 2, DA_QK)
    va = odkv[:, 512:1024].reshape(B, Q, DA_HEADS, DA_V)
    k_all = jnp.concatenate([pd[:, :, 0].reshape(B, -1, DA_HEADS, 2, DA_QK), ka], axis=1)
    v_all = jnp.concatenate([pd[:, :, 1], va], axis=1)
    kpos = jnp.arange(k_all.shape[1])
    lam = jnp.exp(jnp.sum(da_lam[0] * da_lam[1])) - jnp.exp(jnp.sum(da_lam[2] * da_lam[3])) + lam_init
    ya = _diff_attn_core(qa, k_all, v_all, pos, kpos, lam, lam_init, g_sub)
    return ya.reshape(B * Q, DA_HEADS * DA_V)


def kernel(x_prompt, x_sample, cache_diff_kv, cache_nsa_kv, state_nsa_win, state_mlstm_C, state_mlstm_n, state_mlstm_m, page_table, c_prompt, c_sample, norm_mix_g, norm_ffn_g, w_ada, b_ada, w_in, da_lam, da_subln_g, ml_gate_b, ml_norm_g, nsa_cmp_pos, nsa_cmp_w, w_branch, w_bgate, b_bgate, w_out, w_router, b_router, w_up, b_up, w_down, b_down, final_g):
    B, S, D = x_prompt.shape
    Bs, Qs, _ = x_sample.shape
    Tp, Ts = B * S, Bs * Qs
    tm_p = 256
    pos_p = jnp.arange(S)
    pos_s = PAST_LEN + jnp.arange(Qs)
    cos_p, sin_p = _rope_tables(pos_p)
    cos_s, sin_s = _rope_tables(jnp.broadcast_to(pos_s[None, :], (Bs, Qs)).reshape(Ts))

    mods = _ada_all(jnp.concatenate([c_prompt, c_sample], axis=0), w_ada, b_ada)

    xp = x_prompt.reshape(Tp, D)
    xs = x_sample.reshape(Ts, D)
    sp = {k: [] for k in ('diff', 'nsa', 'win', 'C', 'n', 'm')}
    ss = {k: [] for k in ('diff', 'nsa', 'win', 'C', 'n', 'm')}
    for l in range(DEPTH):
        lam_init = 0.8 - 0.6 * math.exp(-0.3 * l)
        w_packed = _pack_w_in_ext(w_in[l])
        wbg = w_bgate[l].astype(BF16)
        wbr = w_branch[l].astype(BF16)
        wout = w_out[l].astype(BF16)
        wr = jnp.pad(w_router[l], ((0, 0), (0, LANES - N_EXPERTS)))
        wrh = wr.astype(BF16)
        wrl = (wr - wrh.astype(F32)).astype(BF16)
        br = jnp.pad(b_router[l], (0, LANES - N_EXPERTS)).reshape(1, LANES)
        mod_p = [m.reshape(B, 1, D) for m in jnp.split(mods[l, :B], 6, axis=-1)]
        mod_s = [m.reshape(1, Ts, D) for m in jnp.split(mods[l, B:], 6, axis=-1)]

        oq, odkv, onkv, owin, oml, osm, odkv_bf, onsa_bf = _in_proj(xp, norm_mix_g[l], mod_p[1], mod_p[0], cos_p, sin_p,
                                                           w_packed, tm_p, S)
        ya = _diff_attn_prompt(oq, odkv_bf, da_lam[l], da_subln_g[l], lam_init, B, S)
        yb, C1, n1, m1 = _mlstm_prompt(oml, osm, ml_gate_b[l], ml_norm_g[l], B, S)
        kc = _nsa_compress_prompt(onkv, nsa_cmp_pos[l], nsa_cmp_w[l], B, S)
        yc = _nsa_prompt(oq, onsa_bf, osm, kc, B, S)
        new_win = owin.reshape(B, S, 2, NSA_GROUPS, NSA_DH)[:, S - min(NSA_WINDOW, S):]
        xp, h2p, lgp = _merge(xp, ya, yb, yc, norm_mix_g[l], (mod_p[1], mod_p[0], mod_p[2], mod_p[4], mod_p[3]),
                              norm_ffn_g[l], wbg, b_bgate[l], wbr, wout, wrh, wrl, br, tm_p, S)
        sp['diff'].append(odkv.reshape(B, S, 2, DA_HEADS, DA_V))
        sp['nsa'].append(onkv.reshape(B, S, 4, NSA_GROUPS, NSA_DH))
        sp['win'].append(new_win)
        sp['C'].append(C1); sp['n'].append(n1); sp['m'].append(m1)

        sc1s, sh1s = mod_s[1].reshape(Ts, D), mod_s[0].reshape(Ts, D)
        u = _in_proj_sample(xs, norm_mix_g[l], sc1s, sh1s, cos_s, sin_s, _pack_w_in(w_in[l], F32))
        oq, odkv, onkv, owin, oml, osm = jnp.split(u, [int(c) for c in np.cumsum(_IN_OUT_W)[:-1]], axis=1)
        ya = _diff_attn_sample_paged(oq, odkv, cache_diff_kv, page_table, l, da_lam[l], da_subln_g[l], lam_init)
        with jax.default_matmul_precision("highest"):
            pn = _gather_pages(cache_nsa_kv, l, page_table)
            yb, C1, n1, m1 = _mlstm_mixer(oml, osm, ml_gate_b[l], ml_norm_g[l], Bs, Qs,
                                          (state_mlstm_C[l], state_mlstm_n[l], state_mlstm_m[l]))
            yc, new_win = _nsa_mixer(oq, onkv, owin, osm, pos_s, nsa_cmp_pos[l], nsa_cmp_w[l], Bs, Qs,
                                     (pn, state_nsa_win[l]))
        mix = _gate_mix_sample(xs, norm_mix_g[l], sc1s, sh1s, ya, yb, yc, w_bgate[l], b_bgate[l], w_branch[l])
        xs, h2s, lgs = _out_sample(xs, mix, mod_s[2].reshape(Ts, D), w_out[l], norm_ffn_g[l],
                                   mod_s[4].reshape(Ts, D), mod_s[3].reshape(Ts, D), wr, br)
        xs = _moe_sample(h2s, lgs[:, :N_EXPERTS], w_up[l], b_up[l], w_down[l], b_down[l], xs,
                         mod_s[5].reshape(Ts, D))
        ss['diff'].append(odkv.reshape(Bs, Qs, 2, DA_HEADS, DA_V))
        ss['nsa'].append(onkv.reshape(Bs, Qs, 4, NSA_GROUPS, NSA_DH))
        ss['win'].append(new_win)
        ss['C'].append(C1); ss['n'].append(n1); ss['m'].append(m1)

        ym = _moe(h2p, lgp[:, :N_EXPERTS], w_up[l], b_up[l], w_down[l], b_down[l])
        xp = xp + jnp.repeat(mod_p[5].reshape(B, D), S, axis=0) * ym

    y_prompt = _final_norm(xp, final_g, tm_p).reshape(B, S, D)
    y_sample = _final_norm(xs, final_g, Ts).reshape(Bs, Qs, D)
    stk = lambda d, k: jnp.stack(d[k], axis=0)
    return (y_prompt, y_sample,
            stk(sp, 'diff'), stk(sp, 'nsa'), stk(sp, 'win'), stk(sp, 'C'), stk(sp, 'n'), stk(sp, 'm'),
            stk(ss, 'diff'), stk(ss, 'nsa'), stk(ss, 'win'), stk(ss, 'C'), stk(ss, 'n'), stk(ss, 'm'))
```

```python
import functools
import math

import jax
import jax.numpy as jnp
import numpy as np
from jax import lax
from jax.experimental import pallas as pl
from jax.experimental.pallas import tpu as pltpu

D_MODEL = 1024
DEPTH = 4
PAST_LEN = 8192
EPS = 1e-6
ROPE_THETA = 10000.0
BRANCH_W = 512
N_BRANCH = 3
DA_HEADS = 4
DA_QK = 64
DA_V = 128
ML_HEADS = 4
ML_QK = 64
ML_V = 128
ML_CHUNK = 64
NSA_HEADS = 8
NSA_GROUPS = 2
NSA_HPG = NSA_HEADS // NSA_GROUPS
NSA_DH = 64
NSA_CMP_LEN = 32
NSA_CMP_STRIDE = 16
NSA_SEL_LEN = 64
NSA_TOPK = 16
NSA_WINDOW = 512
NSA_QBLOCK = 64
FORCE_SCORE = 1.0e4
N_EXPERTS = 32
TOP_K = 4
D_FF = 1024
SWIGLU_LIMIT = 7.0
SWIGLU_ALPHA = 1.702

LANES = 128
VMEM_LIMIT = 56 * 1024 * 1024
F32 = jnp.float32
BF16 = jnp.bfloat16

_C_QA, _C_KA, _C_VA, _C_QM, _C_KM, _C_VM, _C_OM, _C_IM, _C_FM, _C_QN, _C_KVN, _C_GN = (
    0, 512, 1024, 1536, 1792, 2048, 2560, 3072, 3076, 3080, 3592, 4360)

_IN_OUT_W = (1536, 1024, 512, 256, 1536, 128)
_W_IN_PACKED = 4992


def _in_plan():
    plan = []
    col = 0

    def add(width, rope, oi, oc):
        nonlocal col
        step = 256 if width % 256 == 0 else 128
        for s in range(0, width, step):
            plan.append((col + s, step, rope, oi, oc + s))
        col += width

    add(512, True, 0, 0)
    add(512, True, 0, 512)
    add(512, False, 0, 1024)
    add(512, True, 1, 0)
    add(512, False, 1, 512)
    add(128, False, 2, 0)
    add(128, False, 2, 128)
    add(128, True, 2, 256)
    add(128, False, 2, 384)
    add(128, True, 3, 0)
    add(128, False, 3, 128)
    add(256, False, 4, 0)
    add(256, False, 4, 256)
    add(512, False, 4, 512)
    add(512, False, 4, 1024)
    add(128, False, 5, 0)
    assert col == _W_IN_PACKED
    return tuple(plan)


_IN_PLAN = _in_plan()

_NSA_BF_W = 640
_W_IN_EXT = _W_IN_PACKED + NSA_GROUPS * _NSA_BF_W
_IN_PLAN_EXT = tuple(
    (_W_IN_PACKED + g * _NSA_BF_W + c, 128 if c == 512 else 256, c < 512, 6, g * _NSA_BF_W + c)
    for g in range(NSA_GROUPS) for c in (0, 256, 512))


def _pack_w_in_ext(w_in):
    kvn = lambda j, g: w_in[:, _C_KVN + 128 * j + 64 * g:_C_KVN + 128 * j + 64 * (g + 1)].astype(BF16)
    cols = [_pack_w_in(w_in, BF16)]
    for g in range(NSA_GROUPS):
        cols += [kvn(2, g)] * NSA_HPG + [kvn(4, g)] * NSA_HPG + [kvn(3, g), kvn(5, g)]
    return jnp.concatenate(cols, axis=1)


def _pack_w_in(w_in, dtype):
    kvn = lambda j: w_in[:, _C_KVN + 128 * j:_C_KVN + 128 * (j + 1)]
    qn = w_in[:, _C_QN:_C_QN + 512]
    small = jnp.concatenate([w_in[:, _C_IM:_C_IM + 8], w_in[:, _C_GN:_C_GN + 24],
                             jnp.zeros((w_in.shape[0], 96), w_in.dtype)], axis=1)
    cols = [w_in[:, _C_QA:_C_QA + 512], qn, qn, w_in[:, _C_KA:_C_KA + 512], w_in[:, _C_VA:_C_VA + 512],
            kvn(0), kvn(1), kvn(2), kvn(3), kvn(4), kvn(5),
            w_in[:, _C_QM:_C_QM + 256], w_in[:, _C_KM:_C_KM + 256], w_in[:, _C_VM:_C_VM + 512],
            w_in[:, _C_OM:_C_OM + 512], small]
    return jnp.concatenate(cols, axis=1).astype(dtype)


def _rope_tables(pos):
    half = DA_QK // 2
    inv = ROPE_THETA ** (-jnp.arange(half, dtype=F32) / half)
    ang = pos.astype(F32)[:, None] * inv[None, :]
    cos, sin = jnp.cos(ang), jnp.sin(ang)
    cos_t = jnp.concatenate([cos, cos, cos, cos], axis=1)
    sin_t = jnp.concatenate([-sin, sin, -sin, sin], axis=1)
    return cos_t, sin_t


def _norm_mod(x, g, sc, sh):
    y = x * lax.rsqrt(jnp.mean(x * x, axis=-1, keepdims=True) + EPS)
    return (y * g) * (1.0 + sc) + sh


def _ada_body(c_ref, w_ref, b_ref, o_ref):
    c = c_ref[...]
    s = c * jax.nn.sigmoid(c)
    o_ref[0] = jnp.dot(s, w_ref[0], precision=lax.Precision.HIGHEST, preferred_element_type=F32) + b_ref[0]


def _ada_all(c_all, w_ada, b_ada):
    R = c_all.shape[0]
    tn = 512
    return pl.pallas_call(
        _ada_body,
        grid=(DEPTH, 6 * D_MODEL // tn),
        in_specs=[pl.BlockSpec((R, D_MODEL), lambda l, n: (0, 0)),
                  pl.BlockSpec((1, D_MODEL, tn), lambda l, n: (l, 0, n)),
                  pl.BlockSpec((1, 1, tn), lambda l, n: (l, 0, n))],
        out_specs=pl.BlockSpec((1, R, tn), lambda l, n: (l, 0, n)),
        out_shape=jax.ShapeDtypeStruct((DEPTH, R, 6 * D_MODEL), F32),
        compiler_params=pltpu.CompilerParams(vmem_limit_bytes=VMEM_LIMIT),
        name="ada",
    )(c_all, w_ada, b_ada.reshape(DEPTH, 1, 6 * D_MODEL))


def _rope128(y, cos, sin_signed, first_half):
    fwd = pltpu.roll(y, LANES - DA_QK // 2, 1)
    bwd = pltpu.roll(y, DA_QK // 2, 1)
    return y * cos + jnp.where(first_half, fwd, bwd) * sin_signed


def _in_body(x_ref, g_ref, sc_ref, sh_ref, cos_ref, sin_ref, w_ref, oq, odkv, onkv, owin, oml, osm, odkv_bf, onsa_bf):
    outs = (oq, odkv, onkv, owin, oml, osm, onsa_bf)
    h = _norm_mod(x_ref[...], g_ref[...], sc_ref[...], sh_ref[...])
    hb = h.astype(BF16)
    cos = cos_ref[...]
    sin = sin_ref[...]
    lane = lax.broadcasted_iota(jnp.int32, (1, LANES), 1)
    first_half = (lane % DA_QK) < (DA_QK // 2)
    for (c0, width, rope, oi, oc) in _IN_PLAN + _IN_PLAN_EXT:
        y = jnp.dot(hb, w_ref[:, c0:c0 + width], preferred_element_type=F32)
        for s in range(0, width, LANES):
            ys = y[:, s:s + LANES]
            if rope:
                ys = _rope128(ys, cos, sin, first_half)
            outs[oi][:, oc + s:oc + s + LANES] = ys.astype(outs[oi].dtype)
            if oi == 1:
                odkv_bf[:, oc + s:oc + s + LANES] = ys.astype(BF16)


def _in_proj(x, g, sc, sh, cos_t, sin_t, w_packed, tm, rows_per_seq):
    T = x.shape[0]
    nt = T // tm
    tiles_per_seq = max(rows_per_seq // tm, 1)
    n_pos_tiles = cos_t.shape[0] // tm
    R = sc.shape[1]
    mod_spec = pl.BlockSpec((None, R, D_MODEL), lambda i: (i // tiles_per_seq, 0, 0))
    tab_spec = pl.BlockSpec((tm, LANES), lambda i: (i % n_pos_tiles, 0))
    widths = _IN_OUT_W
    bf_widths = (1024, NSA_GROUPS * _NSA_BF_W)
    out_shape = ([jax.ShapeDtypeStruct((T, w), F32) for w in widths]
                 + [jax.ShapeDtypeStruct((T, w), BF16) for w in bf_widths])
    out_specs = [pl.BlockSpec((tm, w), lambda i: (i, 0)) for w in widths + bf_widths]
    return pl.pallas_call(
        _in_body,
        grid=(nt,),
        in_specs=[pl.BlockSpec((tm, D_MODEL), lambda i: (i, 0)),
                  pl.BlockSpec((1, D_MODEL), lambda i: (0, 0)),
                  mod_spec, mod_spec, tab_spec, tab_spec,
                  pl.BlockSpec((D_MODEL, _W_IN_EXT), lambda i: (0, 0))],
        out_specs=out_specs,
        out_shape=out_shape,
        compiler_params=pltpu.CompilerParams(vmem_limit_bytes=VMEM_LIMIT),
        name="in_proj",
    )(x, g.reshape(1, D_MODEL), sc, sh, cos_t, sin_t, w_packed)


def _diff_body(q_ref, kv_ref, lam_ref, g_ref, o_ref, *, lam_init, tq):
    qi = pl.program_id(1)
    lv = lam_ref[...]
    lam = (jnp.exp(jnp.sum(lv[0:1] * lv[1:2], axis=1, keepdims=True))
           - jnp.exp(jnp.sum(lv[2:3] * lv[3:4], axis=1, keepdims=True)) + lam_init)
    lane = lax.broadcasted_iota(jnp.int32, (1, LANES), 1)
    row = lax.broadcasted_iota(jnp.int32, (2 * tq, 1), 0) % tq
    col = lax.broadcasted_iota(jnp.int32, (1, tq), 1)
    causal = col <= row
    for h in range(DA_HEADS):
        qh = q_ref[:, h * LANES:(h + 1) * LANES] * (DA_QK ** -0.5)
        qs = jnp.concatenate([jnp.where(lane < DA_QK, qh, 0.0), jnp.where(lane >= DA_QK, qh, 0.0)],
                             axis=0).astype(BF16)

        def step(kv, carry, masked):
            m, l, acc = carry
            start = pl.multiple_of(kv * tq, tq)
            k = kv_ref[pl.ds(start, tq), h * LANES:(h + 1) * LANES]
            v = kv_ref[pl.ds(start, tq), (DA_HEADS + h) * LANES:(DA_HEADS + h + 1) * LANES]
            s = lax.dot_general(qs, k, (((1,), (1,)), ((), ())), preferred_element_type=F32)
            if masked:
                s = jnp.where(causal, s, -1e30)
            m_new = jnp.maximum(m, jnp.max(s, axis=1, keepdims=True))
            alpha = jnp.exp(m - m_new)
            p = jnp.exp(s - m_new)
            l = alpha * l + jnp.sum(p, axis=1, keepdims=True)
            acc = alpha * acc + jnp.dot(p.astype(BF16), v, preferred_element_type=F32)
            return m_new, l, acc

        init = (jnp.full((2 * tq, 1), -1e30, F32), jnp.zeros((2 * tq, 1), F32), jnp.zeros((2 * tq, LANES), F32))
        carry = lax.fori_loop(0, qi, lambda kv, c: step(kv, c, False), init)
        m, l, acc = step(qi, carry, True)
        o = acc / l
        o = o[:tq] - lam * o[tq:]
        o = o * lax.rsqrt(jnp.mean(o * o, axis=-1, keepdims=True) + EPS) * g_ref[...]
        o_ref[:, h * LANES:(h + 1) * LANES] = o * (1.0 - lam_init)


def _diff_attn_prompt(oq, odkv_bf, da_lam, g_sub, lam_init, B, S, tq=256):
    nq = S // tq
    return pl.pallas_call(
        functools.partial(_diff_body, lam_init=lam_init, tq=tq),
        grid=(B, nq),
        in_specs=[pl.BlockSpec((tq, 512), lambda b, i: (b * nq + i, 0)),
                  pl.BlockSpec((S, 1024), lambda b, i: (b, 0)),
                  pl.BlockSpec((4, DA_QK), lambda b, i: (0, 0)),
                  pl.BlockSpec((1, DA_V), lambda b, i: (0, 0))],
        out_specs=pl.BlockSpec((tq, 512), lambda b, i: (b * nq + i, 0)),
        out_shape=jax.ShapeDtypeStruct((B * S, 512), F32),
        compiler_params=pltpu.CompilerParams(vmem_limit_bytes=VMEM_LIMIT),
        name="diff_attn",
    )(oq, odkv_bf, da_lam, g_sub.reshape(1, DA_V))


_DEC_PAGES = 8


def _diff_dec_body(pt_ref, q_ref, kvn_ref, lam_ref, g_ref, *rest, lam_init, n_steps):
    pages = rest[:_DEC_PAGES]
    o_ref, m_s, l_s, acc_s = rest[_DEC_PAGES:]
    j = pl.program_id(1)
    lane = lax.broadcasted_iota(jnp.int32, (1, LANES), 1)
    maps = (lane < DA_QK, lane >= DA_QK)

    @pl.when(j == 0)
    def _():
        m_s[...] = jnp.full_like(m_s, -1e30)
        l_s[...] = jnp.zeros_like(l_s)
        acc_s[...] = jnp.zeros_like(acc_s)

    def update(r, s, v):
        m_old = m_s[r:r + 1, 0:1]
        m_new = jnp.maximum(m_old, jnp.max(s, axis=0, keepdims=True))
        alpha = jnp.exp(m_old - m_new)
        p = jnp.exp(s - m_new)
        l_s[r:r + 1, :] = alpha * l_s[r:r + 1, :] + jnp.sum(p, axis=0, keepdims=True)
        acc_s[r:r + 1, :] = alpha * acc_s[r:r + 1, :] + jnp.sum(p * v, axis=0, keepdims=True)
        m_s[r:r + 1, :] = jnp.broadcast_to(m_new, (1, LANES))

    qs = [q_ref[:, h * LANES:(h + 1) * LANES] * (DA_QK ** -0.5) for h in range(DA_HEADS)]
    for pg in pages:
        for h in range(DA_HEADS):
            prod = pg[:, 0, h, :] * qs[h]
            v = pg[:, 1, h, :]
            for c in range(2):
                s = jnp.sum(jnp.where(maps[c], prod, 0.0), axis=1, keepdims=True)
                update(2 * h + c, s, v)

    @pl.when(j == n_steps - 1)
    def _():
        lv = lam_ref[...]
        lam = (jnp.exp(jnp.sum(lv[0:1] * lv[1:2], axis=1, keepdims=True))
               - jnp.exp(jnp.sum(lv[2:3] * lv[3:4], axis=1, keepdims=True)) + lam_init)
        for h in range(DA_HEADS):
            prod = kvn_ref[:, h * LANES:(h + 1) * LANES] * qs[h]
            v = kvn_ref[:, (DA_HEADS + h) * LANES:(DA_HEADS + h + 1) * LANES]
            o = []
            for c in range(2):
                r = 2 * h + c
                update(r, jnp.sum(jnp.where(maps[c], prod, 0.0), axis=1, keepdims=True), v)
                o.append(acc_s[r:r + 1, :] / l_s[r:r + 1, :])
            d = o[0] - lam * o[1]
            d = d * lax.rsqrt(jnp.mean(d * d, axis=-1, keepdims=True) + EPS) * g_ref[...]
            o_ref[:, h * LANES:(h + 1) * LANES] = d * (1.0 - lam_init)


def _diff_attn_sample_paged(oq, odkv, cache_diff_kv, page_table, l, da_lam, g_sub, lam_init):
    Bs = oq.shape[0]
    n_pages = page_table.shape[1]
    page = cache_diff_kv.shape[2]
    n_steps = n_pages // _DEC_PAGES
    page_specs = [pl.BlockSpec((None, None, page, 2, DA_HEADS, DA_V),
                               lambda b, j, pt, r=r: (l, pt[b, j * _DEC_PAGES + r], 0, 0, 0, 0))
                  for r in range(_DEC_PAGES)]
    grid_spec = pltpu.PrefetchScalarGridSpec(
        num_scalar_prefetch=1,
        grid=(Bs, n_steps),
        in_specs=[pl.BlockSpec((None, 1, 512), lambda b, j, pt: (b, 0, 0)),
                  pl.BlockSpec((None, 1, 1024), lambda b, j, pt: (b, 0, 0)),
                  pl.BlockSpec((4, DA_QK), lambda b, j, pt: (0, 0)),
                  pl.BlockSpec((1, DA_V), lambda b, j, pt: (0, 0))] + page_specs,
        out_specs=pl.BlockSpec((None, 1, 512), lambda b, j, pt: (b, 0, 0)),
        scratch_shapes=[pltpu.VMEM((2 * DA_HEADS, LANES), F32)] * 3,
    )
    out = pl.pallas_call(
        functools.partial(_diff_dec_body, lam_init=lam_init, n_steps=n_steps),
        grid_spec=grid_spec,
        out_shape=jax.ShapeDtypeStruct((Bs, 1, 512), F32),
        compiler_params=pltpu.CompilerParams(vmem_limit_bytes=VMEM_LIMIT,
                                             dimension_semantics=("arbitrary", "arbitrary")),
        name="diff_attn_sample",
    )(page_table, oq[:, 0:512].reshape(Bs, 1, 512), odkv.reshape(Bs, 1, 1024), da_lam, g_sub.reshape(1, DA_V),
      *([cache_diff_kv] * _DEC_PAGES))
    return out.reshape(Bs, 512)


_ML_L = 128


def _mlstm_body(ml_ref, sm_ref, vt_ref, gb_ref, gn_ref, y_ref, c_out, n_out, m_out, C_s, n_s, m_s, *, nc):
    j = pl.program_id(1)
    L = _ML_L

    @pl.when(j == 0)
    def _():
        C_s[...] = jnp.zeros_like(C_s)
        n_s[...] = jnp.zeros_like(n_s)
        m_s[...] = jnp.zeros_like(m_s)

    rowi = lax.broadcasted_iota(jnp.int32, (L, 1), 0)
    coli = lax.broadcasted_iota(jnp.int32, (1, L), 1)
    lane = lax.broadcasted_iota(jnp.int32, (1, LANES), 1)
    tri = jnp.where(coli <= rowi, 1.0, 0.0)
    smb = sm_ref[...] + gb_ref[...]
    nt_dims = (((1,), (1,)), ((), ()))
    for h in range(ML_HEADS):
        log_i = jnp.sum(jnp.where(lane == h, smb, 0.0), axis=1, keepdims=True)
        log_f = jax.nn.log_sigmoid(jnp.sum(jnp.where(lane == ML_HEADS + h, smb, 0.0), axis=1, keepdims=True))
        gmat = jnp.where(rowi > coli, log_f, 0.0) + jnp.where(rowi == coli, log_i, 0.0)
        d = jnp.dot(tri, gmat, precision=lax.Precision.HIGHEST, preferred_element_type=F32)
        b = d[:, 0:1] - log_i[0:1, :] + log_f[0:1, :]
        m_prev = m_s[h][:, 0:1]
        a = b + m_prev
        d = jnp.where(coli <= rowi, d, -1e30)
        mt = jnp.maximum(a, jnp.max(d, axis=1, keepdims=True))
        w_inter = jnp.exp(a - mt)
        w_intra = jnp.exp(d - mt)
        q = ml_ref[:, h * ML_QK:(h + 1) * ML_QK]
        k = ml_ref[:, 256 + h * ML_QK:256 + (h + 1) * ML_QK] * (ML_QK ** -0.5)
        v = ml_ref[:, 512 + h * ML_V:512 + (h + 1) * ML_V]
        qb, kb = q.astype(BF16), k.astype(BF16)
        sc = w_intra * lax.dot_general(qb, kb, nt_dims, preferred_element_type=F32)
        C = C_s[h]
        n = n_s[h]
        num = (w_inter * lax.dot_general(qb, C.astype(BF16), nt_dims, preferred_element_type=F32)
               + jnp.dot(sc.astype(BF16), v.astype(BF16), preferred_element_type=F32))
        den = w_inter * jnp.sum(q * n, axis=1, keepdims=True) + jnp.sum(sc, axis=1, keepdims=True)
        hh = num / jnp.maximum(jnp.abs(den), jnp.exp(-mt))
        m_new = mt[L - 1:L, :]
        wc = w_inter[L - 1:L, :]
        wi_row = w_intra[L - 1:L, :]
        wi_col = jnp.exp(b[L - 1:L, :] - b + log_i - m_new)
        vt = vt_ref[h * ML_V:(h + 1) * ML_V, :]
        C_s[h] = wc * C + jnp.dot((vt * wi_row).astype(BF16), kb, preferred_element_type=F32)
        n_s[h] = wc * n + jnp.sum(wi_col * k, axis=0, keepdims=True)
        m_s[h] = jnp.broadcast_to(m_new, (1, LANES))
        om = ml_ref[:, 1024 + h * ML_V:1024 + (h + 1) * ML_V]
        y = hh * lax.rsqrt(jnp.mean(hh * hh, axis=-1, keepdims=True) + EPS) * gn_ref[...]
        y_ref[:, h * ML_V:(h + 1) * ML_V] = y * jax.nn.sigmoid(om)

    @pl.when(j == nc - 1)
    def _():
        c_out[...] = C_s[...]
        n_out[...] = n_s[...]
        m_out[...] = m_s[...]


def _mlstm_prompt(oml, osm, gate_b, norm_g, B, S):
    nc = S // _ML_L
    vt = jnp.transpose(oml[:, 512:1024])
    gb = jnp.pad(gate_b, (0, LANES - 2 * ML_HEADS)).reshape(1, LANES)
    y, C1, n1, m1 = pl.pallas_call(
        functools.partial(_mlstm_body, nc=nc),
        grid=(B, nc),
        in_specs=[pl.BlockSpec((_ML_L, 1536), lambda b, j: (b * nc + j, 0)),
                  pl.BlockSpec((_ML_L, LANES), lambda b, j: (b * nc + j, 0)),
                  pl.BlockSpec((ML_HEADS * ML_V, _ML_L), lambda b, j: (0, b * nc + j)),
                  pl.BlockSpec((1, LANES), lambda b, j: (0, 0)),
                  pl.BlockSpec((1, ML_V), lambda b, j: (0, 0))],
        out_specs=[pl.BlockSpec((_ML_L, ML_HEADS * ML_V), lambda b, j: (b * nc + j, 0)),
                   pl.BlockSpec((None, ML_HEADS, ML_V, ML_QK), lambda b, j: (b, 0, 0, 0)),
                   pl.BlockSpec((None, ML_HEADS, 1, ML_QK), lambda b, j: (b, 0, 0, 0)),
                   pl.BlockSpec((None, ML_HEADS, 1, LANES), lambda b, j: (b, 0, 0, 0))],
        out_shape=[jax.ShapeDtypeStruct((B * S, ML_HEADS * ML_V), F32),
                   jax.ShapeDtypeStruct((B, ML_HEADS, ML_V, ML_QK), F32),
                   jax.ShapeDtypeStruct((B, ML_HEADS, 1, ML_QK), F32),
                   jax.ShapeDtypeStruct((B, ML_HEADS, 1, LANES), F32)],
        scratch_shapes=[pltpu.VMEM((ML_HEADS, ML_V, ML_QK), F32), pltpu.VMEM((ML_HEADS, 1, ML_QK), F32),
                        pltpu.VMEM((ML_HEADS, 1, LANES), F32)],
        compiler_params=pltpu.CompilerParams(dimension_semantics=("arbitrary", "arbitrary")),
        name="mlstm",
    )(oml, osm, vt, gb, norm_g.reshape(1, ML_V))
    return y, C1, n1.reshape(B, ML_HEADS, ML_QK), m1[:, :, 0, 0]


_NSA_NC = 256
_NSA_R = NSA_SEL_LEN // NSA_CMP_STRIDE


def _cmp_body(xc_ref, w_ref, pos_ref, wfull_ref, o_ref):
    row = lax.broadcasted_iota(jnp.int32, (_NSA_NC, 1), 0)
    lane = lax.broadcasted_iota(jnp.int32, (1, LANES), 1)
    for kind in range(2):
        const = jnp.dot(pos_ref[kind], wfull_ref[kind], precision=lax.Precision.HIGHEST,
                        preferred_element_type=F32)
        const = jnp.concatenate([const, jnp.zeros_like(const)], axis=1)
        for g in range(NSA_GROUPS):
            pq = jnp.dot(xc_ref[2 * kind + g], w_ref[kind], preferred_element_type=F32)
            nxt = pltpu.roll(pltpu.roll(pq, _NSA_NC - 1, 0), NSA_DH, 1)
            y = pq + nxt + const
            o_ref[2 * kind + g] = jnp.where((row < _NSA_NC - 1) & (lane < NSA_DH), y, 0.0)


def _nsa_compress_prompt(onkv, cmp_pos, cmp_w, B, S):
    n_ch = S // NSA_CMP_STRIDE
    xc = onkv[:, 0:256].reshape(B, n_ch, NSA_CMP_STRIDE, 4, NSA_DH)
    xc = jnp.transpose(xc, (0, 3, 1, 2, 4)).reshape(B, 4, n_ch, NSA_CMP_STRIDE * NSA_DH).astype(BF16)
    half = NSA_CMP_STRIDE * NSA_DH
    w = jnp.concatenate([cmp_w[:, :half], cmp_w[:, half:]], axis=2).astype(BF16)
    return pl.pallas_call(
        _cmp_body,
        grid=(B,),
        in_specs=[pl.BlockSpec((None, 4, n_ch, half), lambda b: (b, 0, 0, 0)),
                  pl.BlockSpec((2, half, LANES), lambda b: (0, 0, 0)),
                  pl.BlockSpec((2, 1, 2 * half), lambda b: (0, 0, 0)),
                  pl.BlockSpec((2, 2 * half, NSA_DH), lambda b: (0, 0, 0))],
        out_specs=pl.BlockSpec((None, 4, n_ch, LANES), lambda b: (b, 0, 0, 0)),
        out_shape=jax.ShapeDtypeStruct((B, 4, n_ch, LANES), F32),
        name="nsa_compress",
    )(xc, w, cmp_pos.reshape(2, 1, 2 * half), cmp_w)


def _nsa_body(qr_ref, qn_ref, kc_ref, vc_ref, kv_ref, sm_ref, e_ref, o_ref, *, tq):
    g = pl.program_id(1)
    qi = pl.program_id(2)
    lane256 = lax.broadcasted_iota(jnp.int32, (1, 2 * LANES), 1)
    lane128 = lax.broadcasted_iota(jnp.int32, (1, LANES), 1)
    lane64 = lax.broadcasted_iota(jnp.int32, (1, NSA_SEL_LEN), 1)
    row = lax.broadcasted_iota(jnp.int32, (tq, 1), 0)
    col = lax.broadcasted_iota(jnp.int32, (1, tq), 1)
    qpos = qi * tq + row
    head_lanes = [(lane256 >= NSA_DH * j) & (lane256 < NSA_DH * (j + 1)) for j in range(NSA_HPG)]
    nt_dims = (((1,), (1,)), ((), ()))

    blk = _NSA_R * (lane256 % NSA_SEL_LEN) + lane256 // NSA_SEL_LEN
    cmask = blk * NSA_CMP_STRIDE + (NSA_CMP_LEN - 1) <= qpos
    qn = qn_ref[...] * (NSA_DH ** -0.5)
    imp = jnp.zeros((tq, _NSA_NC), F32)
    o_cmp = []
    for j in range(NSA_HPG):
        qj = jnp.where(head_lanes[j], qn, 0.0).astype(BF16)
        s = lax.dot_general(qj, kc_ref[...], nt_dims, preferred_element_type=F32)
        s = jnp.where(cmask, s, -1e30)
        e = jnp.where(cmask, jnp.exp(s - jnp.max(s, axis=1, keepdims=True)), 0.0)
        p = e / jnp.maximum(jnp.sum(e, axis=1, keepdims=True), 1e-30)
        imp = imp + p
        o_cmp.append(jnp.dot(p.astype(BF16), vc_ref[...], preferred_element_type=F32))
    imps = (imp[:, 0:64] + imp[:, 64:128]) + (imp[:, 128:192] + imp[:, 192:256])

    qb = qpos // NSA_SEL_LEN
    back = qb - lane64
    score = jnp.where(back == 0, FORCE_SCORE, imps)
    score = jnp.where(back == 1, FORCE_SCORE, score)
    score = jnp.where(back == qb, FORCE_SCORE, score)
    score = jnp.where(back >= 0, score, -FORCE_SCORE)
    rank = jnp.zeros((tq, NSA_SEL_LEN), F32)
    for i in range(NSA_SEL_LEN):
        ci = score[:, i:i + 1]
        ge = jnp.where(ci >= score, 1.0, 0.0)
        gt = jnp.where(ci > score, 1.0, 0.0)
        later = jnp.where(lane64 > i, 1.0, 0.0)
        rank = rank + (gt + later * (ge - gt))
    sel = jnp.where(rank < NSA_TOPK, 1.0, 0.0) * jnp.where(back >= 0, 1.0, 0.0)
    sel = sel.astype(BF16)

    qr = qr_ref[...] * (NSA_DH ** -0.5)
    qjs = [jnp.where(head_lanes[j], qr, 0.0).astype(BF16) for j in range(NSA_HPG)]
    rows4 = NSA_HPG * tq

    def attend(t, carry, kcol, bias):
        m, l, acc = carry
        start = pl.multiple_of(t * tq, tq)
        k = kv_ref[pl.ds(start, tq), kcol:kcol + 2 * LANES]
        v = kv_ref[pl.ds(start, tq), 4 * LANES:5 * LANES]
        s = jnp.concatenate([lax.dot_general(qj, k, nt_dims, preferred_element_type=F32) + bias for qj in qjs],
                            axis=0)
        m_new = jnp.maximum(m, jnp.max(s, axis=1, keepdims=True))
        alpha = jnp.exp(m - m_new)
        p = jnp.exp(s - m_new)
        l = alpha * l + jnp.sum(p, axis=1, keepdims=True)
        acc = alpha * acc + jnp.dot(p.astype(BF16), v, preferred_element_type=F32)
        return m_new, l, acc

    def sel_bias(t, extra=None):
        on = jnp.dot(sel, e_ref[t], preferred_element_type=F32) > 0.5
        inner = 0.0 if extra is None else jnp.where(extra, 0.0, -1e30)
        return jnp.where(on, inner, -1e30)

    init = (jnp.full((rows4, 1), -1e30, F32), jnp.zeros((rows4, 1), F32), jnp.zeros((rows4, LANES), F32))
    causal = col <= row
    carry = lax.fori_loop(0, qi, lambda t, c: attend(t, c, 0, sel_bias(t)), init)
    _, l_s, acc_s = attend(qi, carry, 0, sel_bias(qi, causal))

    neg = jnp.full((tq, tq), -1e30, F32)
    carry = attend(qi, init, 2 * LANES, jnp.where(causal, 0.0, -1e30))
    carry = attend(jnp.maximum(qi - 1, 0), carry, 2 * LANES, jnp.where(qi >= 1, jnp.zeros((tq, tq), F32), neg))
    _, l_w, acc_w = attend(jnp.maximum(qi - 2, 0), carry, 2 * LANES,
                           jnp.where(qi >= 2, jnp.where(col > row, 0.0, -1e30), neg))

    sm = sm_ref[...]
    ys = []
    for j in range(NSA_HPG):
        base = 8 + 3 * (NSA_HPG * g + j)
        gate = [jax.nn.sigmoid(jnp.sum(jnp.where(lane128 == base + c, sm, 0.0), axis=1, keepdims=True))
                for c in range(3)]
        r = slice(j * tq, (j + 1) * tq)
        comb = jnp.where(lane128 < NSA_DH, acc_s[r] / l_s[r] * gate[1], acc_w[r] / l_w[r] * gate[2])
        comb = comb + o_cmp[j] * gate[0]
        ys.append(comb + pltpu.roll(comb, NSA_DH, 1))
    o_ref[:, 0:LANES] = jnp.where(lane128 < NSA_DH, ys[0], ys[1])
    o_ref[:, LANES:2 * LANES] = jnp.where(lane128 < NSA_DH, ys[2], ys[3])


def _nsa_prompt(oq, onsa_bf, osm, kc, B, S, tq=256):
    nq = S // tq
    nsb = S // NSA_SEL_LEN
    perm = jnp.transpose(kc.reshape(B, 4, nsb, _NSA_R, LANES), (0, 1, 3, 2, 4)).reshape(B, 4, _NSA_NC, LANES)
    kc_rep = jnp.tile(perm[:, 0:2, :, 0:NSA_DH], (1, 1, 1, NSA_HPG)).astype(BF16)
    vc = perm[:, 2:4].astype(BF16)
    n_kt = S // tq
    e = (jnp.arange(nsb)[None, :, None]
         == (jnp.arange(n_kt)[:, None, None] * (tq // NSA_SEL_LEN) + jnp.arange(tq)[None, None, :] // NSA_SEL_LEN))
    e = e.astype(BF16)
    return pl.pallas_call(
        functools.partial(_nsa_body, tq=tq),
        grid=(B, NSA_GROUPS, nq),
        in_specs=[pl.BlockSpec((tq, 256), lambda b, g, i: (b * nq + i, 2 + g)),
                  pl.BlockSpec((tq, 256), lambda b, g, i: (b * nq + i, 4 + g)),
                  pl.BlockSpec((None, None, _NSA_NC, 256), lambda b, g, i: (b, g, 0, 0)),
                  pl.BlockSpec((None, None, _NSA_NC, LANES), lambda b, g, i: (b, g, 0, 0)),
                  pl.BlockSpec((S, _NSA_BF_W), lambda b, g, i: (b, g)),
                  pl.BlockSpec((tq, LANES), lambda b, g, i: (b * nq + i, 0)),
                  pl.BlockSpec((n_kt, nsb, tq), lambda b, g, i: (0, 0, 0))],
        out_specs=pl.BlockSpec((tq, 256), lambda b, g, i: (b * nq + i, g)),
        out_shape=jax.ShapeDtypeStruct((B * S, 512), F32),
        compiler_params=pltpu.CompilerParams(vmem_limit_bytes=VMEM_LIMIT),
        name="nsa_attn",
    )(oq, oq, kc_rep, vc, onsa_bf, osm, e)


def _merge_body(x_ref, ya_ref, yb_ref, yc_ref, gm_ref, sc1_ref, sh1_ref, g1_ref, gf_ref, sc2_ref, sh2_ref,
                wbg_ref, bbg_ref, wbr_ref, wout_ref, wrh_ref, wrl_ref, br_ref, xo_ref, h2_ref, lg_ref):
    x = x_ref[...]
    hb = _norm_mod(x, gm_ref[...], sc1_ref[...], sh1_ref[...]).astype(BF16)
    mix = None
    for n, y_ref in enumerate((ya_ref, yb_ref, yc_ref)):
        gate = jax.nn.sigmoid(jnp.dot(hb, wbg_ref[:, n * D_MODEL:(n + 1) * D_MODEL], preferred_element_type=F32)
                              + bbg_ref[:, n * D_MODEL:(n + 1) * D_MODEL])
        proj = jnp.dot(y_ref[...].astype(BF16), wbr_ref[n], preferred_element_type=F32)
        mix = gate * proj if mix is None else mix + gate * proj
    y = jnp.dot(mix.astype(BF16), wout_ref[...], preferred_element_type=F32)
    xn = x + g1_ref[...] * y
    xo_ref[...] = xn
    h2 = _norm_mod(xn, gf_ref[...], sc2_ref[...], sh2_ref[...])
    hi = h2.astype(BF16)
    lo = (h2 - hi.astype(F32)).astype(BF16)
    h2_ref[...] = hi
    lg_ref[...] = (jnp.dot(hi, wrh_ref[...], preferred_element_type=F32)
                   + jnp.dot(lo, wrh_ref[...], preferred_element_type=F32)
                   + jnp.dot(hi, wrl_ref[...], preferred_element_type=F32) + br_ref[...])


def _merge(x, ya, yb, yc, gm, mods, gf, wbg, bbg, wbr, wout, wrh, wrl, br, tm, rows_per_seq):
    T = x.shape[0]
    nt = T // tm
    tiles_per_seq = max(rows_per_seq // tm, 1)
    sc1, sh1, g1, sc2, sh2 = mods
    R = sc1.shape[1]
    row = lambda w: pl.BlockSpec((tm, w), lambda i: (i, 0))
    mod_spec = pl.BlockSpec((None, R, D_MODEL), lambda i: (i // tiles_per_seq, 0, 0))
    const = lambda shape: pl.BlockSpec(shape, lambda i: (0,) * len(shape))
    return pl.pallas_call(
        _merge_body,
        grid=(nt,),
        in_specs=[row(D_MODEL), row(512), row(512), row(512), const((1, D_MODEL)), mod_spec, mod_spec, mod_spec,
                  const((1, D_MODEL)), mod_spec, mod_spec,
                  const((D_MODEL, 3 * D_MODEL)), const((1, 3 * D_MODEL)), const((3, BRANCH_W, D_MODEL)),
                  const((D_MODEL, D_MODEL)), const((D_MODEL, LANES)), const((D_MODEL, LANES)), const((1, LANES))],
        out_specs=[row(D_MODEL), row(D_MODEL), row(LANES)],
        out_shape=[jax.ShapeDtypeStruct((T, D_MODEL), F32), jax.ShapeDtypeStruct((T, D_MODEL), BF16),
                   jax.ShapeDtypeStruct((T, LANES), F32)],
        compiler_params=pltpu.CompilerParams(vmem_limit_bytes=VMEM_LIMIT),
        name="merge",
    )(x, ya, yb, yc, gm.reshape(1, D_MODEL), sc1, sh1, g1, gf.reshape(1, D_MODEL), sc2, sh2,
      wbg, bbg.reshape(1, 3 * D_MODEL), wbr, wout, wrh, wrl, br)


def _moe_body(te_ref, tv_ref, x_ref, wu_ref, bu_ref, wd_ref, bd_ref, o_ref, wu_s, wd_s):
    j = pl.program_id(0)
    e = te_ref[j]
    e_prev = te_ref[jnp.maximum(j - 1, 0)]

    @pl.when((j == 0) | (e != e_prev))
    def _():
        wu_s[...] = wu_ref[0].astype(BF16)
        wd_s[...] = wd_ref[0].astype(BF16)

    @pl.when(tv_ref[j] == 1)
    def _():
        uu = jnp.dot(x_ref[...], wu_s[...], preferred_element_type=F32) + bu_ref[0]
        gl = jnp.minimum(uu[:, :D_FF], SWIGLU_LIMIT)
        up = jnp.clip(uu[:, D_FF:], -SWIGLU_LIMIT, SWIGLU_LIMIT)
        act = gl * jax.nn.sigmoid(SWIGLU_ALPHA * gl) * (up + 1.0)
        o_ref[...] = jnp.dot(act.astype(BF16), wd_s[...], preferred_element_type=F32) + bd_ref[0]

    @pl.when(tv_ref[j] == 0)
    def _():
        o_ref[...] = jnp.zeros_like(o_ref)


def _moe_grouped(x_pad, tile_e, tile_valid, w_up, b_up, w_down, b_down, tm):
    n_tiles = x_pad.shape[0] // tm
    grid_spec = pltpu.PrefetchScalarGridSpec(
        num_scalar_prefetch=2,
        grid=(n_tiles,),
        in_specs=[pl.BlockSpec((tm, D_MODEL), lambda j, te, tv: (j, 0)),
                  pl.BlockSpec((1, D_MODEL, 2 * D_FF), lambda j, te, tv: (te[j], 0, 0)),
                  pl.BlockSpec((1, 1, 2 * D_FF), lambda j, te, tv: (te[j], 0, 0)),
                  pl.BlockSpec((1, D_FF, D_MODEL), lambda j, te, tv: (te[j], 0, 0)),
                  pl.BlockSpec((1, 1, D_MODEL), lambda j, te, tv: (te[j], 0, 0))],
        out_specs=pl.BlockSpec((tm, D_MODEL), lambda j, te, tv: (j, 0)),
        scratch_shapes=[pltpu.VMEM((D_MODEL, 2 * D_FF), BF16), pltpu.VMEM((D_FF, D_MODEL), BF16)],
    )
    return pl.pallas_call(
        _moe_body,
        grid_spec=grid_spec,
        out_shape=jax.ShapeDtypeStruct((x_pad.shape[0], D_MODEL), F32),
        compiler_params=pltpu.CompilerParams(vmem_limit_bytes=VMEM_LIMIT, dimension_semantics=("arbitrary",)),
        name="moe",
    )(tile_e, tile_valid, x_pad, w_up, b_up.reshape(N_EXPERTS, 1, 2 * D_FF), w_down,
      b_down.reshape(N_EXPERTS, 1, D_MODEL))


def _moe(h2, logits, w_up, b_up, w_down, b_down, tm=256):
    T = h2.shape[0]
    A = T * TOP_K
    top_v, top_i = lax.top_k(logits, TOP_K)
    wts = jax.nn.softmax(top_v, axis=-1)
    flat_e = top_i.reshape(A).astype(jnp.int32)
    order = jnp.argsort(flat_e, stable=True).astype(jnp.int32)
    sorted_e = flat_e[order]
    counts = jnp.zeros((N_EXPERTS,), jnp.int32).at[flat_e].add(1)
    tiles_per_e = (counts + tm - 1) // tm
    tile_end = jnp.cumsum(tiles_per_e)
    pad_start = (tile_end - tiles_per_e) * tm
    grp_start = jnp.cumsum(counts) - counts
    pos_sorted = pad_start[sorted_e] + (jnp.arange(A, dtype=jnp.int32) - grp_start[sorted_e])
    n_tiles = (A + N_EXPERTS * (tm - 1)) // tm + 1
    NP = n_tiles * tm
    src_tok = jnp.full((NP,), T, jnp.int32).at[pos_sorted].set(order // TOP_K)
    x_ext = jnp.concatenate([h2, jnp.zeros((1, D_MODEL), h2.dtype)], axis=0)
    x_pad = x_ext[src_tok]
    tile_idx = jnp.arange(n_tiles, dtype=jnp.int32)
    tile_e = jnp.minimum(jnp.searchsorted(tile_end, tile_idx, side='right'), N_EXPERTS - 1).astype(jnp.int32)
    tile_valid = (tile_idx < tile_end[-1]).astype(jnp.int32)
    last_e = tile_e[jnp.maximum(tile_end[-1] - 1, 0)]
    tile_e = jnp.where(tile_valid == 1, tile_e, last_e)
    y_pad = _moe_grouped(x_pad, tile_e, tile_valid, w_up, b_up, w_down, b_down, tm)
    inv_pos = jnp.zeros((A,), jnp.int32).at[order].set(pos_sorted)
    y_sel = y_pad[inv_pos].reshape(T, TOP_K, D_MODEL)
    return jnp.sum(wts[:, :, None] * y_sel, axis=1)


def _dot_hp(a, b):
    return jnp.dot(a, b, precision=lax.Precision.HIGHEST, preferred_element_type=F32)


_IN_ROPE_FLAGS = np.concatenate([np.full((w // LANES,), int(r), np.int32) for (_, w, r, _, _) in _IN_PLAN])


def _in_s_body(flag_ref, x_ref, g_ref, sc_ref, sh_ref, cos_ref, sin_ref, w_ref, o_ref):
    n = pl.program_id(0)
    h = _norm_mod(x_ref[...], g_ref[...], sc_ref[...], sh_ref[...])
    y = _dot_hp(h, w_ref[...])
    lane = lax.broadcasted_iota(jnp.int32, (1, LANES), 1)
    yr = _rope128(y, cos_ref[...], sin_ref[...], (lane % DA_QK) < (DA_QK // 2))
    o_ref[...] = jnp.where(flag_ref[n] == 1, yr, y)


def _in_proj_sample(x, g, sc, sh, cos_t, sin_t, w_packed):
    T = x.shape[0]
    full = lambda shape: pl.BlockSpec(shape, lambda n, f: (0,) * len(shape))
    grid_spec = pltpu.PrefetchScalarGridSpec(
        num_scalar_prefetch=1,
        grid=(_W_IN_PACKED // LANES,),
        in_specs=[full((T, D_MODEL)), full((1, D_MODEL)), full((T, D_MODEL)), full((T, D_MODEL)),
                  full((T, LANES)), full((T, LANES)),
                  pl.BlockSpec((D_MODEL, LANES), lambda n, f: (0, n))],
        out_specs=pl.BlockSpec((T, LANES), lambda n, f: (0, n)),
    )
    return pl.pallas_call(
        _in_s_body, grid_spec=grid_spec,
        out_shape=jax.ShapeDtypeStruct((T, _W_IN_PACKED), F32),
        name="in_proj_sample",
    )(jnp.asarray(_IN_ROPE_FLAGS), x, g.reshape(1, D_MODEL), sc, sh, cos_t, sin_t, w_packed)


def _mix_s_body(x_ref, g_ref, sc_ref, sh_ref, ya_ref, yb_ref, yc_ref, wg0, wg1, wg2, bg0, bg1, bg2, wbr_ref, o_ref):
    h = _norm_mod(x_ref[...], g_ref[...], sc_ref[...], sh_ref[...])
    mix = None
    for n, (y_ref, wg, bg) in enumerate(((ya_ref, wg0, bg0), (yb_ref, wg1, bg1), (yc_ref, wg2, bg2))):
        gate = jax.nn.sigmoid(_dot_hp(h, wg[...]) + bg[...])
        proj = _dot_hp(y_ref[...], wbr_ref[n])
        mix = gate * proj if mix is None else mix + gate * proj
    o_ref[...] = mix


def _gate_mix_sample(x, g, sc, sh, ya, yb, yc, w_bgate, b_bgate, w_branch, tn=256):
    T = x.shape[0]
    nj = D_MODEL // tn
    full = lambda shape: pl.BlockSpec(shape, lambda j: (0,) * len(shape))
    wg = [pl.BlockSpec((D_MODEL, tn), lambda j, n=n: (0, n * nj + j)) for n in range(N_BRANCH)]
    bg = [pl.BlockSpec((1, tn), lambda j, n=n: (0, n * nj + j)) for n in range(N_BRANCH)]
    bb = b_bgate.reshape(1, N_BRANCH * D_MODEL)
    return pl.pallas_call(
        _mix_s_body, grid=(nj,),
        in_specs=[full((T, D_MODEL)), full((1, D_MODEL)), full((T, D_MODEL)), full((T, D_MODEL)),
                  full((T, BRANCH_W)), full((T, BRANCH_W)), full((T, BRANCH_W))] + wg + bg
                 + [pl.BlockSpec((N_BRANCH, BRANCH_W, tn), lambda j: (0, 0, j))],
        out_specs=pl.BlockSpec((T, tn), lambda j: (0, j)),
        out_shape=jax.ShapeDtypeStruct((T, D_MODEL), F32),
        name="gate_mix_sample",
    )(x, g.reshape(1, D_MODEL), sc, sh, ya, yb, yc, w_bgate, w_bgate, w_bgate, bb, bb, bb, w_branch)


def _out_s_body(x_ref, mix_ref, g1_ref, wout_ref, gf_ref, sc2_ref, sh2_ref, wr_ref, br_ref, xo_ref, h2_ref, lg_ref):
    xn = x_ref[...] + g1_ref[...] * _dot_hp(mix_ref[...], wout_ref[...])
    xo_ref[...] = xn
    h2 = _norm_mod(xn, gf_ref[...], sc2_ref[...], sh2_ref[...])
    h2_ref[...] = h2
    lg_ref[...] = _dot_hp(h2, wr_ref[...]) + br_ref[...]


def _out_sample(x, mix, g1, w_out, gf, sc2, sh2, wr, br):
    T = x.shape[0]
    return pl.pallas_call(
        _out_s_body,
        out_shape=[jax.ShapeDtypeStruct((T, D_MODEL), F32), jax.ShapeDtypeStruct((T, D_MODEL), F32),
                   jax.ShapeDtypeStruct((T, LANES), F32)],
        compiler_params=pltpu.CompilerParams(vmem_limit_bytes=VMEM_LIMIT),
        name="out_sample",
    )(x, mix, g1, w_out, gf.reshape(1, D_MODEL), sc2, sh2, wr, br)


def _moe_s_body(h_ref, gate_ref, wu_ref, bu_ref, wd_ref, bd_ref, x_ref, g2_ref, o_ref, acc):
    e = pl.program_id(0)

    @pl.when(e == 0)
    def _():
        acc[...] = jnp.zeros_like(acc)

    uu = _dot_hp(h_ref[...], wu_ref[0]) + bu_ref[0]
    gl = jnp.minimum(uu[:, :D_FF], SWIGLU_LIMIT)
    up = jnp.clip(uu[:, D_FF:], -SWIGLU_LIMIT, SWIGLU_LIMIT)
    act = gl * jax.nn.sigmoid(SWIGLU_ALPHA * gl) * (up + 1.0)
    acc[...] += gate_ref[0] * (_dot_hp(act, wd_ref[0]) + bd_ref[0])

    @pl.when(e == N_EXPERTS - 1)
    def _():
        o_ref[...] = x_ref[...] + g2_ref[...] * acc[...]


def _moe_sample(h2, logits, w_up, b_up, w_down, b_down, x, g2):
    T = h2.shape[0]
    top_v, top_i = lax.top_k(logits, TOP_K)
    wts = jax.nn.softmax(top_v, axis=-1)
    gate = jnp.einsum('tk,tke->et', wts, jax.nn.one_hot(top_i, N_EXPERTS, dtype=F32),
                      precision=lax.Precision.HIGHEST)[:, :, None]
    full = lambda shape: pl.BlockSpec(shape, lambda e: (0,) * len(shape))
    return pl.pallas_call(
        _moe_s_body, grid=(N_EXPERTS,),
        in_specs=[full((T, D_MODEL)), pl.BlockSpec((1, T, 1), lambda e: (e, 0, 0)),
                  pl.BlockSpec((1, D_MODEL, 2 * D_FF), lambda e: (e, 0, 0)),
                  pl.BlockSpec((1, 1, 2 * D_FF), lambda e: (e, 0, 0)),
                  pl.BlockSpec((1, D_FF, D_MODEL), lambda e: (e, 0, 0)),
                  pl.BlockSpec((1, 1, D_MODEL), lambda e: (e, 0, 0)),
                  full((T, D_MODEL)), full((T, D_MODEL))],
        out_specs=full((T, D_MODEL)),
        out_shape=jax.ShapeDtypeStruct((T, D_MODEL), F32),
        scratch_shapes=[pltpu.VMEM((T, D_MODEL), F32)],
        compiler_params=pltpu.CompilerParams(vmem_limit_bytes=VMEM_LIMIT, dimension_semantics=("arbitrary",)),
        name="moe_sample",
    )(h2, gate, w_up, b_up.reshape(N_EXPERTS, 1, 2 * D_FF), w_down, b_down.reshape(N_EXPERTS, 1, D_MODEL), x, g2)


def _final_norm_body(x_ref, g_ref, o_ref):
    x = x_ref[...]
    o_ref[...] = x * lax.rsqrt(jnp.mean(x * x, axis=-1, keepdims=True) + EPS) * g_ref[...]


def _final_norm(x, g, tm):
    T = x.shape[0]
    return pl.pallas_call(
        _final_norm_body,
        grid=(T // tm,),
        in_specs=[pl.BlockSpec((tm, D_MODEL), lambda i: (i, 0)), pl.BlockSpec((1, D_MODEL), lambda i: (0, 0))],
        out_specs=pl.BlockSpec((tm, D_MODEL), lambda i: (i, 0)),
        out_shape=jax.ShapeDtypeStruct((T, D_MODEL), F32),
        name="final_norm",
    )(x, g.reshape(1, D_MODEL))


def _rms_norm(x, g):
    y = x * lax.rsqrt(jnp.mean(x * x, axis=-1, keepdims=True) + EPS)
    return y * g


def _masked_softmax(s, mask):
    s = jnp.where(mask, s, -1e30)
    m = jnp.max(s, axis=-1, keepdims=True)
    e = jnp.where(mask, jnp.exp(s - m), 0.0)
    return e / jnp.maximum(jnp.sum(e, axis=-1, keepdims=True), 1e-30)


def _sweep_queries(fn, qs, qpos, block):
    B, Q = qs[0].shape[:2]
    nb = Q // block

    def split(a):
        return jnp.moveaxis(a.reshape((B, nb, block) + a.shape[2:]), 1, 0)

    out = lax.map(lambda xs: fn(*xs[0], xs[1]), (tuple(split(a) for a in qs), qpos.reshape(nb, block)))
    out = jnp.moveaxis(out, 0, 1)
    return out.reshape((B, Q) + out.shape[3:])


def _diff_attn_core(q, k, v, qpos, kpos, lam, lam_init, g_sub):
    s = jnp.einsum('bqhcd,bkhcd->bhcqk', q, k) * (DA_QK ** -0.5)
    mask = kpos[None, :] <= qpos[:, None]
    p = _masked_softmax(s, mask)
    a = p[:, :, 0] - lam * p[:, :, 1]
    o = jnp.einsum('bhqk,bkhd->bqhd', a, v)
    return _rms_norm(o, g_sub) * (1.0 - lam_init)


def _mlstm_chunked(q, k, v, log_i, log_f, C0, n0, m0):
    B, S, H, DK = q.shape
    L = math.gcd(S, ML_CHUNK)
    nc = S // L

    def to_chunks(a):
        return jnp.moveaxis(a.reshape((B, nc, L) + a.shape[2:]), 1, 0)

    kf = k * (DK ** -0.5)
    causal = jnp.tril(jnp.ones((L, L), dtype=bool))[None, :, :, None]

    def step(carry, xs):
        C, n, m = carry
        qc, kc, vc, ic, fc = xs
        b = jnp.cumsum(fc, axis=1)
        a = b + m[:, None, :]
        D = jnp.where(causal, b[:, :, None, :] - b[:, None, :, :] + ic[:, None, :, :], -jnp.inf)
        mt = jnp.maximum(a, jnp.max(D, axis=2))
        w_inter = jnp.exp(a - mt)
        w_intra = jnp.exp(D - mt[:, :, None, :])
        sc = w_intra * jnp.einsum('bthd,bshd->btsh', qc, kc)
        num = w_inter[..., None] * jnp.einsum('bhvd,bthd->bthv', C, qc) + jnp.einsum('btsh,bshv->bthv', sc, vc)
        den = w_inter * jnp.einsum('bhd,bthd->bth', n, qc) + jnp.sum(sc, axis=2)
        h = num / jnp.maximum(jnp.abs(den), jnp.exp(-mt))[..., None]
        m_new = mt[:, -1]
        wi = w_intra[:, -1]
        wc = w_inter[:, -1]
        C_new = wc[..., None, None] * C + jnp.einsum('bsh,bshv,bshd->bhvd', wi, vc, kc)
        n_new = wc[..., None] * n + jnp.einsum('bsh,bshd->bhd', wi, kc)
        return (C_new, n_new, m_new), h

    (C1, n1, m1), hs = lax.scan(step, (C0, n0, m0),
                                (to_chunks(q), to_chunks(kf), to_chunks(v), to_chunks(log_i), to_chunks(log_f)))
    hs = jnp.moveaxis(hs, 0, 1).reshape(B, S, H, v.shape[-1])
    return hs, C1, n1, m1


def _nsa_compress(rows, pos_emb, w):
    B, T, G, dh = rows.shape
    Tp = -(-T // NSA_SEL_LEN) * NSA_SEL_LEN
    rows = jnp.pad(rows, ((0, 0), (0, Tp - T), (0, 0), (0, 0)))
    ch = rows.reshape(B, Tp // NSA_CMP_STRIDE, NSA_CMP_STRIDE, G, dh)
    n_sub = NSA_CMP_LEN // NSA_CMP_STRIDE
    nc = ch.shape[1] - n_sub + 1
    blocks = jnp.concatenate([ch[:, i:i + nc] for i in range(n_sub)], axis=2)
    blocks = blocks + pos_emb[None, None, :, None, :]
    return jnp.einsum('bnlgd,lde->bnge', blocks, w.reshape(NSA_CMP_LEN, dh, dh))


def _nsa_cmp_branch(q, kc, vc, qpos):
    B, Q, H, dh = q.shape
    nc = kc.shape[1]
    qg = q.reshape(B, Q, NSA_GROUPS, NSA_HPG, dh)
    s = jnp.einsum('bqgjd,bngd->bgjqn', qg, kc) * (dh ** -0.5)
    ends = jnp.arange(nc) * NSA_CMP_STRIDE + NSA_CMP_LEN - 1
    mask = ends[None, :] <= qpos[:, None]
    p = _masked_softmax(s, mask)
    o = jnp.einsum('bgjqn,bngd->bqgjd', p, vc).reshape(B, Q, H, dh)
    return o, jnp.sum(p, axis=2)


def _nsa_select(imp_cmp, qpos, nsb):
    r = NSA_SEL_LEN // NSA_CMP_STRIDE
    imp = jnp.pad(imp_cmp, ((0, 0), (0, 0), (0, 0), (0, nsb * r - imp_cmp.shape[-1])))
    imp = jnp.sum(imp.reshape(imp.shape[:3] + (nsb, r)), axis=-1)
    j = jnp.arange(nsb)[None, :]
    qb = (qpos // NSA_SEL_LEN)[:, None]
    forced = (j == 0) | (j == qb) | (j == qb - 1)
    score = jnp.where(forced, FORCE_SCORE, imp)
    score = jnp.where(j <= qb, score, -FORCE_SCORE)
    _, idx = lax.top_k(score, min(NSA_TOPK, nsb))
    valid = idx <= (qpos // NSA_SEL_LEN)[None, None, :, None]
    return jnp.transpose(idx, (0, 2, 1, 3)), jnp.transpose(valid, (0, 2, 1, 3))


def _to_sel_blocks(rows, nsb):
    B, T, G, dh = rows.shape
    rows = jnp.pad(rows, ((0, 0), (0, nsb * NSA_SEL_LEN - T), (0, 0), (0, 0)))
    return jnp.transpose(rows.reshape(B, nsb, NSA_SEL_LEN, G, dh), (0, 3, 1, 2, 4))


def _nsa_slc_branch(q, idx, valid, qpos, kblk, vblk):
    B, Qb, H, dh = q.shape
    bi = jnp.arange(B)[:, None, None, None]
    gi = jnp.arange(NSA_GROUPS)[None, None, :, None]
    ks = kblk[bi, gi, idx]
    vs = vblk[bi, gi, idx]
    kpos = idx[..., None] * NSA_SEL_LEN + jnp.arange(NSA_SEL_LEN)
    mask = valid[..., None] & (kpos <= qpos[None, :, None, None, None])
    qg = q.reshape(B, Qb, NSA_GROUPS, NSA_HPG, dh)
    s = jnp.einsum('bqgjd,bqgnld->bqgjnl', qg, ks) * (dh ** -0.5)
    p = _masked_softmax(s.reshape(B, Qb, NSA_GROUPS, NSA_HPG, -1), mask.reshape(B, Qb, NSA_GROUPS, 1, -1))
    o = jnp.einsum('bqgjm,bqgmd->bqgjd', p, vs.reshape(B, Qb, NSA_GROUPS, -1, dh))
    return o.reshape(B, Qb, H, dh)


def _nsa_win_branch(q, kw, vw, qpos, kpos):
    B, NB, QB, H, dh = q.shape
    qg = q.reshape(B, NB, QB, NSA_GROUPS, NSA_HPG, dh)
    s = jnp.einsum('bnqgjd,bnkgd->bngjqk', qg, kw) * (dh ** -0.5)
    dpos = qpos[:, :, None] - kpos[:, None, :]
    mask = (dpos >= 0) & (dpos < NSA_WINDOW) & (kpos[:, None, :] >= 0)
    p = _masked_softmax(s, mask[None, :, None, None])
    o = jnp.einsum('bngjqk,bnkgd->bnqgjd', p, vw)
    return o.reshape(B, NB * QB, H, dh)


def _gather_pages(cache, l, page_table):
    g = cache[l, page_table]
    return g.reshape((g.shape[0], g.shape[1] * g.shape[2]) + g.shape[3:])


def _mlstm_mixer(oml, osm, gate_b, norm_g, B, Q, past):
    qm = oml[:, 0:256].reshape(B, Q, ML_HEADS, ML_QK)
    km = oml[:, 256:512].reshape(B, Q, ML_HEADS, ML_QK)
    vm = oml[:, 512:1024].reshape(B, Q, ML_HEADS, ML_V)
    om = oml[:, 1024:1536].reshape(B, Q, ML_HEADS, ML_V)
    im = osm[:, 0:4].reshape(B, Q, ML_HEADS)
    fm = osm[:, 4:8].reshape(B, Q, ML_HEADS)
    log_i = im + gate_b[:ML_HEADS]
    log_f = jax.nn.log_sigmoid(fm + gate_b[ML_HEADS:])
    if past is None:
        C0 = jnp.zeros((B, ML_HEADS, ML_V, ML_QK), F32)
        n0 = jnp.zeros((B, ML_HEADS, ML_QK), F32)
        m0 = jnp.zeros((B, ML_HEADS), F32)
    else:
        C0, n0, m0 = past
    hm, C1, n1, m1 = _mlstm_chunked(qm, km, vm, log_i, log_f, C0, n0, m0)
    hm = _rms_norm(hm, norm_g) * jax.nn.sigmoid(om)
    return hm.reshape(B * Q, ML_HEADS * ML_V), C1, n1, m1


def _nsa_mixer(oq, onkv, owin, osm, pos, cmp_pos, cmp_w, B, Q, past):
    qn_rot = oq[:, 512:1024].reshape(B, Q, NSA_HEADS, NSA_DH)
    qn = oq[:, 1024:1536].reshape(B, Q, NSA_HEADS, NSA_DH)
    nk = onkv.reshape(B, Q, 4, NSA_GROUPS, NSA_DH)
    cmp_k, cmp_v, slc_k, slc_v = nk[:, :, 0], nk[:, :, 1], nk[:, :, 2], nk[:, :, 3]
    wk = owin.reshape(B, Q, 2, NSA_GROUPS, NSA_DH)
    win_k, win_v = wk[:, :, 0], wk[:, :, 1]
    if past is not None:
        pn, buf = past
        cmp_k = jnp.concatenate([pn[:, :, 0], cmp_k], axis=1)
        cmp_v = jnp.concatenate([pn[:, :, 1], cmp_v], axis=1)
        slc_k = jnp.concatenate([pn[:, :, 2], slc_k], axis=1)
        slc_v = jnp.concatenate([pn[:, :, 3], slc_v], axis=1)
    nsb = -(-cmp_k.shape[1] // NSA_SEL_LEN)
    kc = _nsa_compress(cmp_k, cmp_pos[0], cmp_w[0])
    vc = _nsa_compress(cmp_v, cmp_pos[1], cmp_w[1])
    o_cmp, imp = _nsa_cmp_branch(qn, kc, vc, pos)
    idx, valid = _nsa_select(imp, pos, nsb)
    kblk = _to_sel_blocks(slc_k, nsb)
    vblk = _to_sel_blocks(slc_v, nsb)
    o_slc = _sweep_queries(lambda q, ix, ok, p: _nsa_slc_branch(q, ix, ok, p, kblk, vblk),
                           (qn_rot, idx, valid), pos, math.gcd(Q, NSA_QBLOCK))
    if past is None:
        win_len = min(NSA_WINDOW, Q)
        QB = math.gcd(Q, NSA_QBLOCK)
        nb, nw = Q // QB, NSA_WINDOW // QB
        kp = jnp.pad(win_k, ((0, 0), (NSA_WINDOW, 0), (0, 0), (0, 0))).reshape(B, nb + nw, QB, NSA_GROUPS, NSA_DH)
        vp = jnp.pad(win_v, ((0, 0), (NSA_WINDOW, 0), (0, 0), (0, 0))).reshape(B, nb + nw, QB, NSA_GROUPS, NSA_DH)
        band_k = jnp.concatenate([kp[:, i:i + nb] for i in range(nw + 1)], axis=2)
        band_v = jnp.concatenate([vp[:, i:i + nb] for i in range(nw + 1)], axis=2)
        kpos_w = (jnp.arange(nb) * QB)[:, None] - NSA_WINDOW + jnp.arange((nw + 1) * QB)[None, :]
        o_win = _nsa_win_branch(qn_rot.reshape(B, nb, QB, NSA_HEADS, NSA_DH), band_k, band_v,
                                pos.reshape(nb, QB), kpos_w)
        new_win = jnp.stack([win_k, win_v], axis=2)[:, Q - win_len:]
    else:
        win_len = buf.shape[1]
        kw = jnp.concatenate([buf[:, :, 0], win_k], axis=1)
        vw = jnp.concatenate([buf[:, :, 1], win_v], axis=1)
        kpos_w = (pos[0] - win_len) + jnp.arange(win_len + Q)
        o_win = _nsa_win_branch(qn_rot[:, None], kw[:, None], vw[:, None], pos[None, :], kpos_w[None, :])
        new_win = jnp.stack([kw, vw], axis=2)[:, -win_len:]
    gn = jax.nn.sigmoid(osm[:, 8:32].reshape(B, Q, NSA_HEADS, 3))
    yc = (gn[..., 0:1] * o_cmp + gn[..., 1:2] * o_slc + gn[..., 2:3] * o_win).reshape(B * Q, NSA_HEADS * NSA_DH)
    return yc, new_win


def _nsa_sample(oq, onkv, owin, osm, cache_nsa_kv, page_table, l, buf, cmp_pos, cmp_w):
    Bs, n_pages = page_table.shape
    page = cache_nsa_kv.shape[2]
    pos = n_pages * page
    G, HPG, dh = NSA_GROUPS, NSA_HPG, NSA_DH
    qn_rot = oq[:, 512:1024].reshape(Bs, G, HPG, dh)
    qn = oq[:, 1024:1536].reshape(Bs, G, HPG, dh)
    new = onkv.reshape(Bs, 4, G, dh)
    wnew = owin.reshape(Bs, 2, G, dh)
    scale = dh ** -0.5
    n_valid = (pos - (NSA_CMP_LEN - 1)) // NSA_CMP_STRIDE + 1
    n_ch = pos // NSA_CMP_STRIDE
    assert n_valid + 1 <= n_ch
    x = cache_nsa_kv[l][:, :, 0:2][page_table].reshape(Bs, n_ch, NSA_CMP_STRIDE, 2, G, dh)
    w = cmp_w.reshape(2, NSA_CMP_LEN, dh, dh)
    first = jnp.einsum('bcjkgd,kjde->bckge', x, w[:, :NSA_CMP_STRIDE])
    second = jnp.einsum('bcjkgd,kjde->bckge', x, w[:, NSA_CMP_STRIDE:])
    const = jnp.einsum('kld,klde->ke', cmp_pos, w)
    kvc = first[:, :n_valid] + second[:, 1:n_valid + 1] + const[None, None, :, None, :]
    kc, vc = kvc[:, :, 0], kvc[:, :, 1]
    p = jax.nn.softmax(jnp.einsum('bgjd,bngd->bgjn', qn, kc) * scale, axis=-1)
    o_cmp = jnp.einsum('bgjn,bngd->bgjd', p, vc)
    nsb = -(-(pos + 1) // NSA_SEL_LEN)
    r = NSA_SEL_LEN // NSA_CMP_STRIDE
    imp = jnp.pad(jnp.sum(p, axis=2), ((0, 0), (0, 0), (0, nsb * r - n_valid)))
    imp = jnp.sum(imp.reshape(Bs, G, nsb, r), axis=-1)
    qb = pos // NSA_SEL_LEN
    jj = jnp.arange(nsb)
    score = jnp.where((jj == 0) | (jj == qb) | (jj == qb - 1), FORCE_SCORE, imp)
    score = jnp.where(jj <= qb, score, -FORCE_SCORE)
    _, idx = lax.top_k(score, min(NSA_TOPK, nsb))
    n_past_blk = pos // NSA_SEL_LEN
    assert qb == n_past_blk and page % NSA_SEL_LEN == 0
    per_page = page // NSA_SEL_LEN
    idx_c = jnp.minimum(idx, n_past_blk - 1)
    pg = jnp.take_along_axis(page_table[:, None, :], idx_c // per_page, axis=2)
    src = cache_nsa_kv[l].reshape(cache_nsa_kv.shape[1] * per_page, NSA_SEL_LEN, 4, G, dh)
    blk = (pg * per_page + idx_c % per_page)[..., None]
    tok = jnp.arange(NSA_SEL_LEN)[None, None, None, :]
    gi = jnp.arange(G)[None, :, None, None]
    ks = src[blk, tok, 2, gi]
    vs = src[blk, tok, 3, gi]
    s_past = jnp.einsum('bgjd,bgnld->bgjnl', qn_rot, ks) * scale
    s_past = jnp.where((idx < n_past_blk)[:, :, None, :, None], s_past, -1e30).reshape(Bs, G, HPG, -1)
    s_new = jnp.einsum('bgjd,bgd->bgj', qn_rot, new[:, 2]) * scale
    s_all = jnp.concatenate([s_past, s_new[..., None]], axis=-1)
    p_all = jax.nn.softmax(s_all, axis=-1)
    o_slc = (jnp.einsum('bgjm,bgmd->bgjd', p_all[..., :-1], vs.reshape(Bs, G, -1, dh))
             + p_all[..., -1:] * new[:, 3][:, :, None, :])
    win_len = buf.shape[1]
    kw = jnp.concatenate([buf[:, :, 0], wnew[:, None, 0]], axis=1)
    vw = jnp.concatenate([buf[:, :, 1], wnew[:, None, 1]], axis=1)
    dpos = win_len - jnp.arange(win_len + 1)
    s_w = jnp.einsum('bgjd,bkgd->bgjk', qn_rot, kw) * scale
    p_w = jax.nn.softmax(jnp.where(dpos < NSA_WINDOW, s_w, -1e30), axis=-1)
    o_win = jnp.einsum('bgjk,bkgd->bgjd', p_w, vw)
    new_win = jnp.stack([kw, vw], axis=2)[:, -win_len:]
    gn = jax.nn.sigmoid(osm[:, 8:32].reshape(Bs, G, HPG, 3))
    yc = gn[..., 0:1] * o_cmp + gn[..., 1:2] * o_slc + gn[..., 2:3] * o_win
    return yc.reshape(Bs, NSA_HEADS * dh), new_win


def _diff_attn_sample(oq, odkv, pd, pos, da_lam, g_sub, lam_init, B, Q):
    qa = oq[:, 0:512].reshape(B, Q, DA_HEADS, 2, DA_QK)
    ka = odkv[:, 0:512].reshape(B, Q, DA_HEADS, 2, DA_QK)
    va = odkv[:, 512:1024].reshape(B, Q, DA_HEADS, DA_V)
    k_all = jnp.concatenate([pd[:, :, 0].reshape(B, -1, DA_HEADS, 2, DA_QK), ka], axis=1)
    v_all = jnp.concatenate([pd[:, :, 1], va], axis=1)
    kpos = jnp.arange(k_all.shape[1])
    lam = jnp.exp(jnp.sum(da_lam[0] * da_lam[1])) - jnp.exp(jnp.sum(da_lam[2] * da_lam[3])) + lam_init
    ya = _diff_attn_core(qa, k_all, v_all, pos, kpos, lam, lam_init, g_sub)
    return ya.reshape(B * Q, DA_HEADS * DA_V)


def kernel(x_prompt, x_sample, cache_diff_kv, cache_nsa_kv, state_nsa_win, state_mlstm_C, state_mlstm_n, state_mlstm_m, page_table, c_prompt, c_sample, norm_mix_g, norm_ffn_g, w_ada, b_ada, w_in, da_lam, da_subln_g, ml_gate_b, ml_norm_g, nsa_cmp_pos, nsa_cmp_w, w_branch, w_bgate, b_bgate, w_out, w_router, b_router, w_up, b_up, w_down, b_down, final_g):
    B, S, D = x_prompt.shape
    Bs, Qs, _ = x_sample.shape
    Tp, Ts = B * S, Bs * Qs
    tm_p = 256
    pos_p = jnp.arange(S)
    pos_s = PAST_LEN + jnp.arange(Qs)
    cos_p, sin_p = _rope_tables(pos_p)
    cos_s, sin_s = _rope_tables(jnp.broadcast_to(pos_s[None, :], (Bs, Qs)).reshape(Ts))

    mods = _ada_all(jnp.concatenate([c_prompt, c_sample], axis=0), w_ada, b_ada)

    xp = x_prompt.reshape(Tp, D)
    xs = x_sample.reshape(Ts, D)
    sp = {k: [] for k in ('diff', 'nsa', 'win', 'C', 'n', 'm')}
    ss = {k: [] for k in ('diff', 'nsa', 'win', 'C', 'n', 'm')}
    for l in range(DEPTH):
        lam_init = 0.8 - 0.6 * math.exp(-0.3 * l)
        w_packed = _pack_w_in_ext(w_in[l])
        wbg = w_bgate[l].astype(BF16)
        wbr = w_branch[l].astype(BF16)
        wout = w_out[l].astype(BF16)
        wr = jnp.pad(w_router[l], ((0, 0), (0, LANES - N_EXPERTS)))
        wrh = wr.astype(BF16)
        wrl = (wr - wrh.astype(F32)).astype(BF16)
        br = jnp.pad(b_router[l], (0, LANES - N_EXPERTS)).reshape(1, LANES)
        mod_p = [m.reshape(B, 1, D) for m in jnp.split(mods[l, :B], 6, axis=-1)]
        mod_s = [m.reshape(1, Ts, D) for m in jnp.split(mods[l, B:], 6, axis=-1)]

        oq, odkv, onkv, owin, oml, osm, odkv_bf, onsa_bf = _in_proj(xp, norm_mix_g[l], mod_p[1], mod_p[0], cos_p, sin_p,
                                                           w_packed, tm_p, S)
        ya = _diff_attn_prompt(oq, odkv_bf, da_lam[l], da_subln_g[l], lam_init, B, S)
        yb, C1, n1, m1 = _mlstm_prompt(oml, osm, ml_gate_b[l], ml_norm_g[l], B, S)
        kc = _nsa_compress_prompt(onkv, nsa_cmp_pos[l], nsa_cmp_w[l], B, S)
        yc = _nsa_prompt(oq, onsa_bf, osm, kc, B, S)
        new_win = owin.reshape(B, S, 2, NSA_GROUPS, NSA_DH)[:, S - min(NSA_WINDOW, S):]
        xp, h2p, lgp = _merge(xp, ya, yb, yc, norm_mix_g[l], (mod_p[1], mod_p[0], mod_p[2], mod_p[4], mod_p[3]),
                              norm_ffn_g[l], wbg, b_bgate[l], wbr, wout, wrh, wrl, br, tm_p, S)
        sp['diff'].append(odkv.reshape(B, S, 2, DA_HEADS, DA_V))
        sp['nsa'].append(onkv.reshape(B, S, 4, NSA_GROUPS, NSA_DH))
        sp['win'].append(new_win)
        sp['C'].append(C1); sp['n'].append(n1); sp['m'].append(m1)

        sc1s, sh1s = mod_s[1].reshape(Ts, D), mod_s[0].reshape(Ts, D)
        u = _in_proj_sample(xs, norm_mix_g[l], sc1s, sh1s, cos_s, sin_s, _pack_w_in(w_in[l], F32))
        oq, odkv, onkv, owin, oml, osm = jnp.split(u, [int(c) for c in np.cumsum(_IN_OUT_W)[:-1]], axis=1)
        ya = _diff_attn_sample_paged(oq, odkv, cache_diff_kv, page_table, l, da_lam[l], da_subln_g[l], lam_init)
        with jax.default_matmul_precision("highest"):
            yb, C1, n1, m1 = _mlstm_mixer(oml, osm, ml_gate_b[l], ml_norm_g[l], Bs, Qs,
                                          (state_mlstm_C[l], state_mlstm_n[l], state_mlstm_m[l]))
            yc, new_win = _nsa_sample(oq, onkv, owin, osm, cache_nsa_kv, page_table, l, state_nsa_win[l],
                                      nsa_cmp_pos[l], nsa_cmp_w[l])
        mix = _gate_mix_sample(xs, norm_mix_g[l], sc1s, sh1s, ya, yb, yc, w_bgate[l], b_bgate[l], w_branch[l])
        xs, h2s, lgs = _out_sample(xs, mix, mod_s[2].reshape(Ts, D), w_out[l], norm_ffn_g[l],
                                   mod_s[4].reshape(Ts, D), mod_s[3].reshape(Ts, D), wr, br)
        xs = _moe_sample(h2s, lgs[:, :N_EXPERTS], w_up[l], b_up[l], w_down[l], b_down[l], xs,
                         mod_s[5].reshape(Ts, D))
        ss['diff'].append(odkv.reshape(Bs, Qs, 2, DA_HEADS, DA_V))
        ss['nsa'].append(onkv.reshape(Bs, Qs, 4, NSA_GROUPS, NSA_DH))
        ss['win'].append(new_win)
        ss['C'].append(C1); ss['n'].append(n1); ss['m'].append(m1)

        ym = _moe(h2p, lgp[:, :N_EXPERTS], w_up[l], b_up[l], w_down[l], b_down[l])
        xp = xp + jnp.repeat(mod_p[5].reshape(B, D), S, axis=0) * ym

    y_prompt = _final_norm(xp, final_g, tm_p).reshape(B, S, D)
    y_sample = _final_norm(xs, final_g, Ts).reshape(Bs, Qs, D)
    stk = lambda d, k: jnp.stack(d[k], axis=0)
    return (y_prompt, y_sample,
            stk(sp, 'diff'), stk(sp, 'nsa'), stk(sp, 'win'), stk(sp, 'C'), stk(sp, 'n'), stk(sp, 'm'),
            stk(ss, 'diff'), stk(ss, 'nsa'), stk(ss, 'win'), stk(ss, 'C'), stk(ss, 'n'), stk(ss, 'm'))
```

```python
import functools
import math

import jax
import jax.numpy as jnp
import numpy as np
from jax import lax
from jax.experimental import pallas as pl
from jax.experimental.pallas import tpu as pltpu

D_MODEL = 1024
DEPTH = 4
PAST_LEN = 8192
EPS = 1e-6
ROPE_THETA = 10000.0
BRANCH_W = 512
N_BRANCH = 3
DA_HEADS = 4
DA_QK = 64
DA_V = 128
ML_HEADS = 4
ML_QK = 64
ML_V = 128
ML_CHUNK = 64
NSA_HEADS = 8
NSA_GROUPS = 2
NSA_HPG = NSA_HEADS // NSA_GROUPS
NSA_DH = 64
NSA_CMP_LEN = 32
NSA_CMP_STRIDE = 16
NSA_SEL_LEN = 64
NSA_TOPK = 16
NSA_WINDOW = 512
NSA_QBLOCK = 64
FORCE_SCORE = 1.0e4
N_EXPERTS = 32
TOP_K = 4
D_FF = 1024
SWIGLU_LIMIT = 7.0
SWIGLU_ALPHA = 1.702

LANES = 128
VMEM_LIMIT = 56 * 1024 * 1024
F32 = jnp.float32
BF16 = jnp.bfloat16

_C_QA, _C_KA, _C_VA, _C_QM, _C_KM, _C_VM, _C_OM, _C_IM, _C_FM, _C_QN, _C_KVN, _C_GN = (
    0, 512, 1024, 1536, 1792, 2048, 2560, 3072, 3076, 3080, 3592, 4360)

_IN_OUT_W = (1536, 1024, 512, 256, 1536, 128)
_W_IN_PACKED = 4992


def _in_plan():
    plan = []
    col = 0

    def add(width, rope, oi, oc):
        nonlocal col
        step = 256 if width % 256 == 0 else 128
        for s in range(0, width, step):
            plan.append((col + s, step, rope, oi, oc + s))
        col += width

    add(512, True, 0, 0)
    add(512, True, 0, 512)
    add(512, False, 0, 1024)
    add(512, True, 1, 0)
    add(512, False, 1, 512)
    add(128, False, 2, 0)
    add(128, False, 2, 128)
    add(128, True, 2, 256)
    add(128, False, 2, 384)
    add(128, True, 3, 0)
    add(128, False, 3, 128)
    add(256, False, 4, 0)
    add(256, False, 4, 256)
    add(512, False, 4, 512)
    add(512, False, 4, 1024)
    add(128, False, 5, 0)
    assert col == _W_IN_PACKED
    return tuple(plan)


_IN_PLAN = _in_plan()

_NSA_BF_W = 640
_W_IN_EXT = _W_IN_PACKED + NSA_GROUPS * _NSA_BF_W
_IN_PLAN_EXT = tuple(
    (_W_IN_PACKED + g * _NSA_BF_W + c, 128 if c == 512 else 256, c < 512, 6, g * _NSA_BF_W + c)
    for g in range(NSA_GROUPS) for c in (0, 256, 512))


def _pack_w_in_ext(w_in):
    kvn = lambda j, g: w_in[:, _C_KVN + 128 * j + 64 * g:_C_KVN + 128 * j + 64 * (g + 1)].astype(BF16)
    cols = [_pack_w_in(w_in, BF16)]
    for g in range(NSA_GROUPS):
        cols += [kvn(2, g)] * NSA_HPG + [kvn(4, g)] * NSA_HPG + [kvn(3, g), kvn(5, g)]
    return jnp.concatenate(cols, axis=1)


def _pack_w_in(w_in, dtype):
    kvn = lambda j: w_in[:, _C_KVN + 128 * j:_C_KVN + 128 * (j + 1)]
    qn = w_in[:, _C_QN:_C_QN + 512]
    small = jnp.concatenate([w_in[:, _C_IM:_C_IM + 8], w_in[:, _C_GN:_C_GN + 24],
                             jnp.zeros((w_in.shape[0], 96), w_in.dtype)], axis=1)
    cols = [w_in[:, _C_QA:_C_QA + 512], qn, qn, w_in[:, _C_KA:_C_KA + 512], w_in[:, _C_VA:_C_VA + 512],
            kvn(0), kvn(1), kvn(2), kvn(3), kvn(4), kvn(5),
            w_in[:, _C_QM:_C_QM + 256], w_in[:, _C_KM:_C_KM + 256], w_in[:, _C_VM:_C_VM + 512],
            w_in[:, _C_OM:_C_OM + 512], small]
    return jnp.concatenate(cols, axis=1).astype(dtype)


def _rope_tables(pos):
    half = DA_QK // 2
    inv = ROPE_THETA ** (-jnp.arange(half, dtype=F32) / half)
    ang = pos.astype(F32)[:, None] * inv[None, :]
    cos, sin = jnp.cos(ang), jnp.sin(ang)
    cos_t = jnp.concatenate([cos, cos, cos, cos], axis=1)
    sin_t = jnp.concatenate([-sin, sin, -sin, sin], axis=1)
    return cos_t, sin_t


def _norm_mod(x, g, sc, sh):
    y = x * lax.rsqrt(jnp.mean(x * x, axis=-1, keepdims=True) + EPS)
    return (y * g) * (1.0 + sc) + sh


def _ada_body(c_ref, w_ref, b_ref, o_ref):
    c = c_ref[...]
    s = c * jax.nn.sigmoid(c)
    o_ref[0] = jnp.dot(s, w_ref[0], precision=lax.Precision.HIGHEST, preferred_element_type=F32) + b_ref[0]


def _ada_all(c_all, w_ada, b_ada):
    R = c_all.shape[0]
    tn = 512
    return pl.pallas_call(
        _ada_body,
        grid=(DEPTH, 6 * D_MODEL // tn),
        in_specs=[pl.BlockSpec((R, D_MODEL), lambda l, n: (0, 0)),
                  pl.BlockSpec((1, D_MODEL, tn), lambda l, n: (l, 0, n)),
                  pl.BlockSpec((1, 1, tn), lambda l, n: (l, 0, n))],
        out_specs=pl.BlockSpec((1, R, tn), lambda l, n: (l, 0, n)),
        out_shape=jax.ShapeDtypeStruct((DEPTH, R, 6 * D_MODEL), F32),
        compiler_params=pltpu.CompilerParams(vmem_limit_bytes=VMEM_LIMIT),
        name="ada",
    )(c_all, w_ada, b_ada.reshape(DEPTH, 1, 6 * D_MODEL))


def _rope128(y, cos, sin_signed, first_half):
    fwd = pltpu.roll(y, LANES - DA_QK // 2, 1)
    bwd = pltpu.roll(y, DA_QK // 2, 1)
    return y * cos + jnp.where(first_half, fwd, bwd) * sin_signed


def _in_body(x_ref, g_ref, sc_ref, sh_ref, cos_ref, sin_ref, w_ref, oq, odkv, onkv, owin, oml, osm, odkv_bf, onsa_bf):
    outs = (oq, odkv, onkv, owin, oml, osm, onsa_bf)
    h = _norm_mod(x_ref[...], g_ref[...], sc_ref[...], sh_ref[...])
    hb = h.astype(BF16)
    cos = cos_ref[...]
    sin = sin_ref[...]
    lane = lax.broadcasted_iota(jnp.int32, (1, LANES), 1)
    first_half = (lane % DA_QK) < (DA_QK // 2)
    for (c0, width, rope, oi, oc) in _IN_PLAN + _IN_PLAN_EXT:
        y = jnp.dot(hb, w_ref[:, c0:c0 + width], preferred_element_type=F32)
        for s in range(0, width, LANES):
            ys = y[:, s:s + LANES]
            if rope:
                ys = _rope128(ys, cos, sin, first_half)
            outs[oi][:, oc + s:oc + s + LANES] = ys.astype(outs[oi].dtype)
            if oi == 1:
                odkv_bf[:, oc + s:oc + s + LANES] = ys.astype(BF16)


def _in_proj(x, g, sc, sh, cos_t, sin_t, w_packed, tm, rows_per_seq):
    T = x.shape[0]
    nt = T // tm
    tiles_per_seq = max(rows_per_seq // tm, 1)
    n_pos_tiles = cos_t.shape[0] // tm
    R = sc.shape[1]
    mod_spec = pl.BlockSpec((None, R, D_MODEL), lambda i: (i // tiles_per_seq, 0, 0))
    tab_spec = pl.BlockSpec((tm, LANES), lambda i: (i % n_pos_tiles, 0))
    widths = _IN_OUT_W
    bf_widths = (1024, NSA_GROUPS * _NSA_BF_W)
    out_shape = ([jax.ShapeDtypeStruct((T, w), F32) for w in widths]
                 + [jax.ShapeDtypeStruct((T, w), BF16) for w in bf_widths])
    out_specs = [pl.BlockSpec((tm, w), lambda i: (i, 0)) for w in widths + bf_widths]
    return pl.pallas_call(
        _in_body,
        grid=(nt,),
        in_specs=[pl.BlockSpec((tm, D_MODEL), lambda i: (i, 0)),
                  pl.BlockSpec((1, D_MODEL), lambda i: (0, 0)),
                  mod_spec, mod_spec, tab_spec, tab_spec,
                  pl.BlockSpec((D_MODEL, _W_IN_EXT), lambda i: (0, 0))],
        out_specs=out_specs,
        out_shape=out_shape,
        compiler_params=pltpu.CompilerParams(vmem_limit_bytes=VMEM_LIMIT),
        name="in_proj",
    )(x, g.reshape(1, D_MODEL), sc, sh, cos_t, sin_t, w_packed)


def _diff_body(q_ref, kv_ref, lam_ref, g_ref, o_ref, *, lam_init, tq):
    qi = pl.program_id(1)
    lv = lam_ref[...]
    lam = (jnp.exp(jnp.sum(lv[0:1] * lv[1:2], axis=1, keepdims=True))
           - jnp.exp(jnp.sum(lv[2:3] * lv[3:4], axis=1, keepdims=True)) + lam_init)
    lane = lax.broadcasted_iota(jnp.int32, (1, LANES), 1)
    row = lax.broadcasted_iota(jnp.int32, (2 * tq, 1), 0) % tq
    col = lax.broadcasted_iota(jnp.int32, (1, tq), 1)
    causal = col <= row
    qss = []
    for h in range(DA_HEADS):
        qh = q_ref[:, h * LANES:(h + 1) * LANES] * (DA_QK ** -0.5)
        qss.append(jnp.concatenate([jnp.where(lane < DA_QK, qh, 0.0), jnp.where(lane >= DA_QK, qh, 0.0)],
                                   axis=0).astype(BF16))

    def step(kv, carry, masked):
        start = pl.multiple_of(kv * tq, tq)
        out = []
        for h in range(DA_HEADS):
            m, l, acc = carry[h]
            k = kv_ref[pl.ds(start, tq), h * LANES:(h + 1) * LANES]
            v = kv_ref[pl.ds(start, tq), (DA_HEADS + h) * LANES:(DA_HEADS + h + 1) * LANES]
            s = lax.dot_general(qss[h], k, (((1,), (1,)), ((), ())), preferred_element_type=F32)
            if masked:
                s = jnp.where(causal, s, -1e30)
            m_new = jnp.maximum(m, jnp.max(s, axis=1, keepdims=True))
            alpha = jnp.exp(m - m_new)
            p = jnp.exp(s - m_new)
            l = alpha * l + jnp.sum(p, axis=1, keepdims=True)
            acc = alpha * acc + jnp.dot(p.astype(BF16), v, preferred_element_type=F32)
            out.append((m_new, l, acc))
        return tuple(out)

    init = tuple((jnp.full((2 * tq, 1), -1e30, F32), jnp.zeros((2 * tq, 1), F32), jnp.zeros((2 * tq, LANES), F32))
                 for _ in range(DA_HEADS))
    carry = lax.fori_loop(0, qi, lambda kv, c: step(kv, c, False), init)
    carry = step(qi, carry, True)
    for h in range(DA_HEADS):
        m, l, acc = carry[h]
        o = acc / l
        o = o[:tq] - lam * o[tq:]
        o = o * lax.rsqrt(jnp.mean(o * o, axis=-1, keepdims=True) + EPS) * g_ref[...]
        o_ref[:, h * LANES:(h + 1) * LANES] = o * (1.0 - lam_init)


def _diff_attn_prompt(oq, odkv_bf, da_lam, g_sub, lam_init, B, S, tq=256):
    nq = S // tq
    return pl.pallas_call(
        functools.partial(_diff_body, lam_init=lam_init, tq=tq),
        grid=(B, nq),
        in_specs=[pl.BlockSpec((tq, 512), lambda b, i: (b * nq + i, 0)),
                  pl.BlockSpec((S, 1024), lambda b, i: (b, 0)),
                  pl.BlockSpec((4, DA_QK), lambda b, i: (0, 0)),
                  pl.BlockSpec((1, DA_V), lambda b, i: (0, 0))],
        out_specs=pl.BlockSpec((tq, 512), lambda b, i: (b * nq + i, 0)),
        out_shape=jax.ShapeDtypeStruct((B * S, 512), F32),
        compiler_params=pltpu.CompilerParams(vmem_limit_bytes=VMEM_LIMIT),
        name="diff_attn",
    )(oq, odkv_bf, da_lam, g_sub.reshape(1, DA_V))


_DEC_PAGES = 8


def _diff_dec_body(pt_ref, q_ref, kvn_ref, lam_ref, g_ref, *rest, lam_init, n_steps):
    pages = rest[:_DEC_PAGES]
    o_ref, m_s, l_s, acc_s = rest[_DEC_PAGES:]
    j = pl.program_id(1)
    lane = lax.broadcasted_iota(jnp.int32, (1, LANES), 1)
    maps = (lane < DA_QK, lane >= DA_QK)

    @pl.when(j == 0)
    def _():
        m_s[...] = jnp.full_like(m_s, -1e30)
        l_s[...] = jnp.zeros_like(l_s)
        acc_s[...] = jnp.zeros_like(acc_s)

    def update(r, s, v):
        m_old = m_s[r:r + 1, 0:1]
        m_new = jnp.maximum(m_old, jnp.max(s, axis=0, keepdims=True))
        alpha = jnp.exp(m_old - m_new)
        p = jnp.exp(s - m_new)
        l_s[r:r + 1, :] = alpha * l_s[r:r + 1, :] + jnp.sum(p, axis=0, keepdims=True)
        acc_s[r:r + 1, :] = alpha * acc_s[r:r + 1, :] + jnp.sum(p * v, axis=0, keepdims=True)
        m_s[r:r + 1, :] = jnp.broadcast_to(m_new, (1, LANES))

    qs = [q_ref[:, h * LANES:(h + 1) * LANES] * (DA_QK ** -0.5) for h in range(DA_HEADS)]
    for pg in pages:
        for h in range(DA_HEADS):
            prod = pg[:, 0, h, :] * qs[h]
            v = pg[:, 1, h, :]
            for c in range(2):
                s = jnp.sum(jnp.where(maps[c], prod, 0.0), axis=1, keepdims=True)
                update(2 * h + c, s, v)

    @pl.when(j == n_steps - 1)
    def _():
        lv = lam_ref[...]
        lam = (jnp.exp(jnp.sum(lv[0:1] * lv[1:2], axis=1, keepdims=True))
               - jnp.exp(jnp.sum(lv[2:3] * lv[3:4], axis=1, keepdims=True)) + lam_init)
        for h in range(DA_HEADS):
            prod = kvn_ref[:, h * LANES:(h + 1) * LANES] * qs[h]
            v = kvn_ref[:, (DA_HEADS + h) * LANES:(DA_HEADS + h + 1) * LANES]
            o = []
            for c in range(2):
                r = 2 * h + c
                update(r, jnp.sum(jnp.where(maps[c], prod, 0.0), axis=1, keepdims=True), v)
                o.append(acc_s[r:r + 1, :] / l_s[r:r + 1, :])
            d = o[0] - lam * o[1]
            d = d * lax.rsqrt(jnp.mean(d * d, axis=-1, keepdims=True) + EPS) * g_ref[...]
            o_ref[:, h * LANES:(h + 1) * LANES] = d * (1.0 - lam_init)


def _diff_attn_sample_paged(oq, odkv, cache_diff_kv, page_table, l, da_lam, g_sub, lam_init):
    Bs = oq.shape[0]
    n_pages = page_table.shape[1]
    page = cache_diff_kv.shape[2]
    n_steps = n_pages // _DEC_PAGES
    page_specs = [pl.BlockSpec((None, None, page, 2, DA_HEADS, DA_V),
                               lambda b, j, pt, r=r: (l, pt[b, j * _DEC_PAGES + r], 0, 0, 0, 0))
                  for r in range(_DEC_PAGES)]
    grid_spec = pltpu.PrefetchScalarGridSpec(
        num_scalar_prefetch=1,
        grid=(Bs, n_steps),
        in_specs=[pl.BlockSpec((None, 1, 512), lambda b, j, pt: (b, 0, 0)),
                  pl.BlockSpec((None, 1, 1024), lambda b, j, pt: (b, 0, 0)),
                  pl.BlockSpec((4, DA_QK), lambda b, j, pt: (0, 0)),
                  pl.BlockSpec((1, DA_V), lambda b, j, pt: (0, 0))] + page_specs,
        out_specs=pl.BlockSpec((None, 1, 512), lambda b, j, pt: (b, 0, 0)),
        scratch_shapes=[pltpu.VMEM((2 * DA_HEADS, LANES), F32)] * 3,
    )
    out = pl.pallas_call(
        functools.partial(_diff_dec_body, lam_init=lam_init, n_steps=n_steps),
        grid_spec=grid_spec,
        out_shape=jax.ShapeDtypeStruct((Bs, 1, 512), F32),
        compiler_params=pltpu.CompilerParams(vmem_limit_bytes=VMEM_LIMIT,
                                             dimension_semantics=("arbitrary", "arbitrary")),
        name="diff_attn_sample",
    )(page_table, oq[:, 0:512].reshape(Bs, 1, 512), odkv.reshape(Bs, 1, 1024), da_lam, g_sub.reshape(1, DA_V),
      *([cache_diff_kv] * _DEC_PAGES))
    return out.reshape(Bs, 512)


_ML_L = 128


def _mlstm_body(ml_ref, sm_ref, vt_ref, gb_ref, gn_ref, y_ref, c_out, n_out, m_out, C_s, n_s, m_s, *, nc):
    j = pl.program_id(1)
    L = _ML_L

    @pl.when(j == 0)
    def _():
        C_s[...] = jnp.zeros_like(C_s)
        n_s[...] = jnp.zeros_like(n_s)
        m_s[...] = jnp.zeros_like(m_s)

    rowi = lax.broadcasted_iota(jnp.int32, (L, 1), 0)
    coli = lax.broadcasted_iota(jnp.int32, (1, L), 1)
    lane = lax.broadcasted_iota(jnp.int32, (1, LANES), 1)
    tri = jnp.where(coli <= rowi, 1.0, 0.0)
    smb = sm_ref[...] + gb_ref[...]
    nt_dims = (((1,), (1,)), ((), ()))
    for h in range(ML_HEADS):
        log_i = jnp.sum(jnp.where(lane == h, smb, 0.0), axis=1, keepdims=True)
        log_f = jax.nn.log_sigmoid(jnp.sum(jnp.where(lane == ML_HEADS + h, smb, 0.0), axis=1, keepdims=True))
        gmat = jnp.where(rowi > coli, log_f, 0.0) + jnp.where(rowi == coli, log_i, 0.0)
        d = jnp.dot(tri, gmat, precision=lax.Precision.HIGHEST, preferred_element_type=F32)
        b = d[:, 0:1] - log_i[0:1, :] + log_f[0:1, :]
        m_prev = m_s[h][:, 0:1]
        a = b + m_prev
        d = jnp.where(coli <= rowi, d, -1e30)
        mt = jnp.maximum(a, jnp.max(d, axis=1, keepdims=True))
        w_inter = jnp.exp(a - mt)
        w_intra = jnp.exp(d - mt)
        q = ml_ref[:, h * ML_QK:(h + 1) * ML_QK]
        k = ml_ref[:, 256 + h * ML_QK:256 + (h + 1) * ML_QK] * (ML_QK ** -0.5)
        v = ml_ref[:, 512 + h * ML_V:512 + (h + 1) * ML_V]
        qb, kb = q.astype(BF16), k.astype(BF16)
        sc = w_intra * lax.dot_general(qb, kb, nt_dims, preferred_element_type=F32)
        C = C_s[h]
        n = n_s[h]
        num = (w_inter * lax.dot_general(qb, C.astype(BF16), nt_dims, preferred_element_type=F32)
               + jnp.dot(sc.astype(BF16), v.astype(BF16), preferred_element_type=F32))
        den = w_inter * jnp.sum(q * n, axis=1, keepdims=True) + jnp.sum(sc, axis=1, keepdims=True)
        hh = num / jnp.maximum(jnp.abs(den), jnp.exp(-mt))
        m_new = mt[L - 1:L, :]
        wc = w_inter[L - 1:L, :]
        wi_row = w_intra[L - 1:L, :]
        wi_col = jnp.exp(b[L - 1:L, :] - b + log_i - m_new)
        vt = vt_ref[h * ML_V:(h + 1) * ML_V, :]
        C_s[h] = wc * C + jnp.dot((vt * wi_row).astype(BF16), kb, preferred_element_type=F32)
        n_s[h] = wc * n + jnp.sum(wi_col * k, axis=0, keepdims=True)
        m_s[h] = jnp.broadcast_to(m_new, (1, LANES))
        om = ml_ref[:, 1024 + h * ML_V:1024 + (h + 1) * ML_V]
        y = hh * lax.rsqrt(jnp.mean(hh * hh, axis=-1, keepdims=True) + EPS) * gn_ref[...]
        y_ref[:, h * ML_V:(h + 1) * ML_V] = y * jax.nn.sigmoid(om)

    @pl.when(j == nc - 1)
    def _():
        c_out[...] = C_s[...]
        n_out[...] = n_s[...]
        m_out[...] = m_s[...]


def _mlstm_prompt(oml, osm, gate_b, norm_g, B, S):
    nc = S // _ML_L
    vt = jnp.transpose(oml[:, 512:1024])
    gb = jnp.pad(gate_b, (0, LANES - 2 * ML_HEADS)).reshape(1, LANES)
    y, C1, n1, m1 = pl.pallas_call(
        functools.partial(_mlstm_body, nc=nc),
        grid=(B, nc),
        in_specs=[pl.BlockSpec((_ML_L, 1536), lambda b, j: (b * nc + j, 0)),
                  pl.BlockSpec((_ML_L, LANES), lambda b, j: (b * nc + j, 0)),
                  pl.BlockSpec((ML_HEADS * ML_V, _ML_L), lambda b, j: (0, b * nc + j)),
                  pl.BlockSpec((1, LANES), lambda b, j: (0, 0)),
                  pl.BlockSpec((1, ML_V), lambda b, j: (0, 0))],
        out_specs=[pl.BlockSpec((_ML_L, ML_HEADS * ML_V), lambda b, j: (b * nc + j, 0)),
                   pl.BlockSpec((None, ML_HEADS, ML_V, ML_QK), lambda b, j: (b, 0, 0, 0)),
                   pl.BlockSpec((None, ML_HEADS, 1, ML_QK), lambda b, j: (b, 0, 0, 0)),
                   pl.BlockSpec((None, ML_HEADS, 1, LANES), lambda b, j: (b, 0, 0, 0))],
        out_shape=[jax.ShapeDtypeStruct((B * S, ML_HEADS * ML_V), F32),
                   jax.ShapeDtypeStruct((B, ML_HEADS, ML_V, ML_QK), F32),
                   jax.ShapeDtypeStruct((B, ML_HEADS, 1, ML_QK), F32),
                   jax.ShapeDtypeStruct((B, ML_HEADS, 1, LANES), F32)],
        scratch_shapes=[pltpu.VMEM((ML_HEADS, ML_V, ML_QK), F32), pltpu.VMEM((ML_HEADS, 1, ML_QK), F32),
                        pltpu.VMEM((ML_HEADS, 1, LANES), F32)],
        compiler_params=pltpu.CompilerParams(dimension_semantics=("arbitrary", "arbitrary")),
        name="mlstm",
    )(oml, osm, vt, gb, norm_g.reshape(1, ML_V))
    return y, C1, n1.reshape(B, ML_HEADS, ML_QK), m1[:, :, 0, 0]


_NSA_NC = 256
_NSA_R = NSA_SEL_LEN // NSA_CMP_STRIDE


def _cmp_body(xc_ref, w_ref, pos_ref, wfull_ref, o_ref):
    row = lax.broadcasted_iota(jnp.int32, (_NSA_NC, 1), 0)
    lane = lax.broadcasted_iota(jnp.int32, (1, LANES), 1)
    for kind in range(2):
        const = jnp.dot(pos_ref[kind], wfull_ref[kind], precision=lax.Precision.HIGHEST,
                        preferred_element_type=F32)
        const = jnp.concatenate([const, jnp.zeros_like(const)], axis=1)
        for g in range(NSA_GROUPS):
            pq = jnp.dot(xc_ref[2 * kind + g], w_ref[kind], preferred_element_type=F32)
            nxt = pltpu.roll(pltpu.roll(pq, _NSA_NC - 1, 0), NSA_DH, 1)
            y = pq + nxt + const
            o_ref[2 * kind + g] = jnp.where((row < _NSA_NC - 1) & (lane < NSA_DH), y, 0.0)


def _nsa_compress_prompt(onkv, cmp_pos, cmp_w, B, S):
    n_ch = S // NSA_CMP_STRIDE
    xc = onkv[:, 0:256].reshape(B, n_ch, NSA_CMP_STRIDE, 4, NSA_DH)
    xc = jnp.transpose(xc, (0, 3, 1, 2, 4)).reshape(B, 4, n_ch, NSA_CMP_STRIDE * NSA_DH).astype(BF16)
    half = NSA_CMP_STRIDE * NSA_DH
    w = jnp.concatenate([cmp_w[:, :half], cmp_w[:, half:]], axis=2).astype(BF16)
    return pl.pallas_call(
        _cmp_body,
        grid=(B,),
        in_specs=[pl.BlockSpec((None, 4, n_ch, half), lambda b: (b, 0, 0, 0)),
                  pl.BlockSpec((2, half, LANES), lambda b: (0, 0, 0)),
                  pl.BlockSpec((2, 1, 2 * half), lambda b: (0, 0, 0)),
                  pl.BlockSpec((2, 2 * half, NSA_DH), lambda b: (0, 0, 0))],
        out_specs=pl.BlockSpec((None, 4, n_ch, LANES), lambda b: (b, 0, 0, 0)),
        out_shape=jax.ShapeDtypeStruct((B, 4, n_ch, LANES), F32),
        name="nsa_compress",
    )(xc, w, cmp_pos.reshape(2, 1, 2 * half), cmp_w)


def _nsa_body(qr_ref, qn_ref, kc_ref, vc_ref, kv_ref, sm_ref, e_ref, o_ref, *, tq):
    g = pl.program_id(1)
    qi = pl.program_id(2)
    lane256 = lax.broadcasted_iota(jnp.int32, (1, 2 * LANES), 1)
    lane128 = lax.broadcasted_iota(jnp.int32, (1, LANES), 1)
    lane64 = lax.broadcasted_iota(jnp.int32, (1, NSA_SEL_LEN), 1)
    row = lax.broadcasted_iota(jnp.int32, (tq, 1), 0)
    col = lax.broadcasted_iota(jnp.int32, (1, tq), 1)
    qpos = qi * tq + row
    head_lanes = [(lane256 >= NSA_DH * j) & (lane256 < NSA_DH * (j + 1)) for j in range(NSA_HPG)]
    nt_dims = (((1,), (1,)), ((), ()))

    blk = _NSA_R * (lane256 % NSA_SEL_LEN) + lane256 // NSA_SEL_LEN
    cmask = blk * NSA_CMP_STRIDE + (NSA_CMP_LEN - 1) <= qpos
    qn = qn_ref[...] * (NSA_DH ** -0.5)
    imp = jnp.zeros((tq, _NSA_NC), F32)
    o_cmp = []
    for j in range(NSA_HPG):
        qj = jnp.where(head_lanes[j], qn, 0.0).astype(BF16)
        s = lax.dot_general(qj, kc_ref[...], nt_dims, preferred_element_type=F32)
        s = jnp.where(cmask, s, -1e30)
        e = jnp.where(cmask, jnp.exp(s - jnp.max(s, axis=1, keepdims=True)), 0.0)
        p = e / jnp.maximum(jnp.sum(e, axis=1, keepdims=True), 1e-30)
        imp = imp + p
        o_cmp.append(jnp.dot(p.astype(BF16), vc_ref[...], preferred_element_type=F32))
    imps = (imp[:, 0:64] + imp[:, 64:128]) + (imp[:, 128:192] + imp[:, 192:256])

    qb = qpos // NSA_SEL_LEN
    back = qb - lane64
    score = jnp.where(back == 0, FORCE_SCORE, imps)
    score = jnp.where(back == 1, FORCE_SCORE, score)
    score = jnp.where(back == qb, FORCE_SCORE, score)
    score = jnp.where(back >= 0, score, -FORCE_SCORE)
    rank = jnp.zeros((tq, NSA_SEL_LEN), F32)
    for i in range(NSA_SEL_LEN):
        ci = score[:, i:i + 1]
        ge = jnp.where(ci >= score, 1.0, 0.0)
        gt = jnp.where(ci > score, 1.0, 0.0)
        later = jnp.where(lane64 > i, 1.0, 0.0)
        rank = rank + (gt + later * (ge - gt))
    sel = jnp.where(rank < NSA_TOPK, 1.0, 0.0) * jnp.where(back >= 0, 1.0, 0.0)
    sel = sel.astype(BF16)

    qr = qr_ref[...] * (NSA_DH ** -0.5)
    qjs = [jnp.where(head_lanes[j], qr, 0.0).astype(BF16) for j in range(NSA_HPG)]
    rows4 = NSA_HPG * tq

    def attend(t, carry, kcol, bias):
        m, l, acc = carry
        start = pl.multiple_of(t * tq, tq)
        k = kv_ref[pl.ds(start, tq), kcol:kcol + 2 * LANES]
        v = kv_ref[pl.ds(start, tq), 4 * LANES:5 * LANES]
        s = jnp.concatenate([lax.dot_general(qj, k, nt_dims, preferred_element_type=F32) + bias for qj in qjs],
                            axis=0)
        m_new = jnp.maximum(m, jnp.max(s, axis=1, keepdims=True))
        alpha = jnp.exp(m - m_new)
        p = jnp.exp(s - m_new)
        l = alpha * l + jnp.sum(p, axis=1, keepdims=True)
        acc = alpha * acc + jnp.dot(p.astype(BF16), v, preferred_element_type=F32)
        return m_new, l, acc

    def sel_bias(t, extra=None):
        on = jnp.dot(sel, e_ref[t], preferred_element_type=F32) > 0.5
        inner = 0.0 if extra is None else jnp.where(extra, 0.0, -1e30)
        return jnp.where(on, inner, -1e30)

    init = (jnp.full((rows4, 1), -1e30, F32), jnp.zeros((rows4, 1), F32), jnp.zeros((rows4, LANES), F32))
    causal = col <= row
    carry = lax.fori_loop(0, qi, lambda t, c: attend(t, c, 0, sel_bias(t)), init)
    _, l_s, acc_s = attend(qi, carry, 0, sel_bias(qi, causal))

    neg = jnp.full((tq, tq), -1e30, F32)
    carry = attend(qi, init, 2 * LANES, jnp.where(causal, 0.0, -1e30))
    carry = attend(jnp.maximum(qi - 1, 0), carry, 2 * LANES, jnp.where(qi >= 1, jnp.zeros((tq, tq), F32), neg))
    _, l_w, acc_w = attend(jnp.maximum(qi - 2, 0), carry, 2 * LANES,
                           jnp.where(qi >= 2, jnp.where(col > row, 0.0, -1e30), neg))

    sm = sm_ref[...]
    ys = []
    for j in range(NSA_HPG):
        base = 8 + 3 * (NSA_HPG * g + j)
        gate = [jax.nn.sigmoid(jnp.sum(jnp.where(lane128 == base + c, sm, 0.0), axis=1, keepdims=True))
                for c in range(3)]
        r = slice(j * tq, (j + 1) * tq)
        comb = jnp.where(lane128 < NSA_DH, acc_s[r] / l_s[r] * gate[1], acc_w[r] / l_w[r] * gate[2])
        comb = comb + o_cmp[j] * gate[0]
        ys.append(comb + pltpu.roll(comb, NSA_DH, 1))
    o_ref[:, 0:LANES] = jnp.where(lane128 < NSA_DH, ys[0], ys[1])
    o_ref[:, LANES:2 * LANES] = jnp.where(lane128 < NSA_DH, ys[2], ys[3])


def _nsa_prompt(oq, onsa_bf, osm, kc, B, S, tq=256):
    nq = S // tq
    nsb = S // NSA_SEL_LEN
    perm = jnp.transpose(kc.reshape(B, 4, nsb, _NSA_R, LANES), (0, 1, 3, 2, 4)).reshape(B, 4, _NSA_NC, LANES)
    kc_rep = jnp.tile(perm[:, 0:2, :, 0:NSA_DH], (1, 1, 1, NSA_HPG)).astype(BF16)
    vc = perm[:, 2:4].astype(BF16)
    n_kt = S // tq
    e = (jnp.arange(nsb)[None, :, None]
         == (jnp.arange(n_kt)[:, None, None] * (tq // NSA_SEL_LEN) + jnp.arange(tq)[None, None, :] // NSA_SEL_LEN))
    e = e.astype(BF16)
    return pl.pallas_call(
        functools.partial(_nsa_body, tq=tq),
        grid=(B, NSA_GROUPS, nq),
        in_specs=[pl.BlockSpec((tq, 256), lambda b, g, i: (b * nq + i, 2 + g)),
                  pl.BlockSpec((tq, 256), lambda b, g, i: (b * nq + i, 4 + g)),
                  pl.BlockSpec((None, None, _NSA_NC, 256), lambda b, g, i: (b, g, 0, 0)),
                  pl.BlockSpec((None, None, _NSA_NC, LANES), lambda b, g, i: (b, g, 0, 0)),
                  pl.BlockSpec((S, _NSA_BF_W), lambda b, g, i: (b, g)),
                  pl.BlockSpec((tq, LANES), lambda b, g, i: (b * nq + i, 0)),
                  pl.BlockSpec((n_kt, nsb, tq), lambda b, g, i: (0, 0, 0))],
        out_specs=pl.BlockSpec((tq, 256), lambda b, g, i: (b * nq + i, g)),
        out_shape=jax.ShapeDtypeStruct((B * S, 512), F32),
        compiler_params=pltpu.CompilerParams(vmem_limit_bytes=VMEM_LIMIT),
        name="nsa_attn",
    )(oq, oq, kc_rep, vc, onsa_bf, osm, e)


def _merge_body(x_ref, ya_ref, yb_ref, yc_ref, gm_ref, sc1_ref, sh1_ref, g1_ref, gf_ref, sc2_ref, sh2_ref,
                wbg_ref, bbg_ref, wbr_ref, wout_ref, wrh_ref, wrl_ref, br_ref, xo_ref, h2_ref, lg_ref):
    x = x_ref[...]
    hb = _norm_mod(x, gm_ref[...], sc1_ref[...], sh1_ref[...]).astype(BF16)
    mix = None
    for n, y_ref in enumerate((ya_ref, yb_ref, yc_ref)):
        gate = jax.nn.sigmoid(jnp.dot(hb, wbg_ref[:, n * D_MODEL:(n + 1) * D_MODEL], preferred_element_type=F32)
                              + bbg_ref[:, n * D_MODEL:(n + 1) * D_MODEL])
        proj = jnp.dot(y_ref[...].astype(BF16), wbr_ref[n], preferred_element_type=F32)
        mix = gate * proj if mix is None else mix + gate * proj
    y = jnp.dot(mix.astype(BF16), wout_ref[...], preferred_element_type=F32)
    xn = x + g1_ref[...] * y
    xo_ref[...] = xn
    h2 = _norm_mod(xn, gf_ref[...], sc2_ref[...], sh2_ref[...])
    hi = h2.astype(BF16)
    lo = (h2 - hi.astype(F32)).astype(BF16)
    h2_ref[...] = hi
    lg_ref[...] = (jnp.dot(hi, wrh_ref[...], preferred_element_type=F32)
                   + jnp.dot(lo, wrh_ref[...], preferred_element_type=F32)
                   + jnp.dot(hi, wrl_ref[...], preferred_element_type=F32) + br_ref[...])


def _merge(x, ya, yb, yc, gm, mods, gf, wbg, bbg, wbr, wout, wrh, wrl, br, tm, rows_per_seq):
    T = x.shape[0]
    nt = T // tm
    tiles_per_seq = max(rows_per_seq // tm, 1)
    sc1, sh1, g1, sc2, sh2 = mods
    R = sc1.shape[1]
    row = lambda w: pl.BlockSpec((tm, w), lambda i: (i, 0))
    mod_spec = pl.BlockSpec((None, R, D_MODEL), lambda i: (i // tiles_per_seq, 0, 0))
    const = lambda shape: pl.BlockSpec(shape, lambda i: (0,) * len(shape))
    return pl.pallas_call(
        _merge_body,
        grid=(nt,),
        in_specs=[row(D_MODEL), row(512), row(512), row(512), const((1, D_MODEL)), mod_spec, mod_spec, mod_spec,
                  const((1, D_MODEL)), mod_spec, mod_spec,
                  const((D_MODEL, 3 * D_MODEL)), const((1, 3 * D_MODEL)), const((3, BRANCH_W, D_MODEL)),
                  const((D_MODEL, D_MODEL)), const((D_MODEL, LANES)), const((D_MODEL, LANES)), const((1, LANES))],
        out_specs=[row(D_MODEL), row(D_MODEL), row(LANES)],
        out_shape=[jax.ShapeDtypeStruct((T, D_MODEL), F32), jax.ShapeDtypeStruct((T, D_MODEL), BF16),
                   jax.ShapeDtypeStruct((T, LANES), F32)],
        compiler_params=pltpu.CompilerParams(vmem_limit_bytes=VMEM_LIMIT),
        name="merge",
    )(x, ya, yb, yc, gm.reshape(1, D_MODEL), sc1, sh1, g1, gf.reshape(1, D_MODEL), sc2, sh2,
      wbg, bbg.reshape(1, 3 * D_MODEL), wbr, wout, wrh, wrl, br)


def _moe_body(te_ref, tv_ref, x_ref, wu_ref, bu_ref, wd_ref, bd_ref, o_ref, wu_s, wd_s):
    j = pl.program_id(0)
    e = te_ref[j]
    e_prev = te_ref[jnp.maximum(j - 1, 0)]

    @pl.when((j == 0) | (e != e_prev))
    def _():
        wu_s[...] = wu_ref[0].astype(BF16)
        wd_s[...] = wd_ref[0].astype(BF16)

    @pl.when(tv_ref[j] == 1)
    def _():
        uu = jnp.dot(x_ref[...], wu_s[...], preferred_element_type=F32) + bu_ref[0]
        gl = jnp.minimum(uu[:, :D_FF], SWIGLU_LIMIT)
        up = jnp.clip(uu[:, D_FF:], -SWIGLU_LIMIT, SWIGLU_LIMIT)
        act = gl * jax.nn.sigmoid(SWIGLU_ALPHA * gl) * (up + 1.0)
        o_ref[...] = jnp.dot(act.astype(BF16), wd_s[...], preferred_element_type=F32) + bd_ref[0]

    @pl.when(tv_ref[j] == 0)
    def _():
        o_ref[...] = jnp.zeros_like(o_ref)


def _moe_grouped(x_pad, tile_e, tile_valid, w_up, b_up, w_down, b_down, tm):
    n_tiles = x_pad.shape[0] // tm
    grid_spec = pltpu.PrefetchScalarGridSpec(
        num_scalar_prefetch=2,
        grid=(n_tiles,),
        in_specs=[pl.BlockSpec((tm, D_MODEL), lambda j, te, tv: (j, 0)),
                  pl.BlockSpec((1, D_MODEL, 2 * D_FF), lambda j, te, tv: (te[j], 0, 0)),
                  pl.BlockSpec((1, 1, 2 * D_FF), lambda j, te, tv: (te[j], 0, 0)),
                  pl.BlockSpec((1, D_FF, D_MODEL), lambda j, te, tv: (te[j], 0, 0)),
                  pl.BlockSpec((1, 1, D_MODEL), lambda j, te, tv: (te[j], 0, 0))],
        out_specs=pl.BlockSpec((tm, D_MODEL), lambda j, te, tv: (j, 0)),
        scratch_shapes=[pltpu.VMEM((D_MODEL, 2 * D_FF), BF16), pltpu.VMEM((D_FF, D_MODEL), BF16)],
    )
    return pl.pallas_call(
        _moe_body,
        grid_spec=grid_spec,
        out_shape=jax.ShapeDtypeStruct((x_pad.shape[0], D_MODEL), F32),
        compiler_params=pltpu.CompilerParams(vmem_limit_bytes=VMEM_LIMIT, dimension_semantics=("arbitrary",)),
        name="moe",
    )(tile_e, tile_valid, x_pad, w_up, b_up.reshape(N_EXPERTS, 1, 2 * D_FF), w_down,
      b_down.reshape(N_EXPERTS, 1, D_MODEL))


def _moe(h2, logits, w_up, b_up, w_down, b_down, tm=256):
    T = h2.shape[0]
    A = T * TOP_K
    top_v, top_i = lax.top_k(logits, TOP_K)
    wts = jax.nn.softmax(top_v, axis=-1)
    flat_e = top_i.reshape(A).astype(jnp.int32)
    order = jnp.argsort(flat_e, stable=True).astype(jnp.int32)
    sorted_e = flat_e[order]
    counts = jnp.zeros((N_EXPERTS,), jnp.int32).at[flat_e].add(1)
    tiles_per_e = (counts + tm - 1) // tm
    tile_end = jnp.cumsum(tiles_per_e)
    pad_start = (tile_end - tiles_per_e) * tm
    grp_start = jnp.cumsum(counts) - counts
    pos_sorted = pad_start[sorted_e] + (jnp.arange(A, dtype=jnp.int32) - grp_start[sorted_e])
    n_tiles = (A + N_EXPERTS * (tm - 1)) // tm + 1
    NP = n_tiles * tm
    src_tok = jnp.full((NP,), T, jnp.int32).at[pos_sorted].set(order // TOP_K)
    x_ext = jnp.concatenate([h2, jnp.zeros((1, D_MODEL), h2.dtype)], axis=0)
    x_pad = x_ext[src_tok]
    tile_idx = jnp.arange(n_tiles, dtype=jnp.int32)
    tile_e = jnp.minimum(jnp.searchsorted(tile_end, tile_idx, side='right'), N_EXPERTS - 1).astype(jnp.int32)
    tile_valid = (tile_idx < tile_end[-1]).astype(jnp.int32)
    last_e = tile_e[jnp.maximum(tile_end[-1] - 1, 0)]
    tile_e = jnp.where(tile_valid == 1, tile_e, last_e)
    y_pad = _moe_grouped(x_pad, tile_e, tile_valid, w_up, b_up, w_down, b_down, tm)
    inv_pos = jnp.zeros((A,), jnp.int32).at[order].set(pos_sorted)
    y_sel = y_pad[inv_pos].reshape(T, TOP_K, D_MODEL)
    return jnp.sum(wts[:, :, None] * y_sel, axis=1)


def _dot_hp(a, b):
    return jnp.dot(a, b, precision=lax.Precision.HIGHEST, preferred_element_type=F32)


_IN_ROPE_FLAGS = np.concatenate([np.full((w // LANES,), int(r), np.int32) for (_, w, r, _, _) in _IN_PLAN])


def _in_s_body(flag_ref, x_ref, g_ref, sc_ref, sh_ref, cos_ref, sin_ref, w_ref, o_ref):
    n = pl.program_id(0)
    h = _norm_mod(x_ref[...], g_ref[...], sc_ref[...], sh_ref[...])
    y = _dot_hp(h, w_ref[...])
    lane = lax.broadcasted_iota(jnp.int32, (1, LANES), 1)
    yr = _rope128(y, cos_ref[...], sin_ref[...], (lane % DA_QK) < (DA_QK // 2))
    o_ref[...] = jnp.where(flag_ref[n] == 1, yr, y)


def _in_proj_sample(x, g, sc, sh, cos_t, sin_t, w_packed):
    T = x.shape[0]
    full = lambda shape: pl.BlockSpec(shape, lambda n, f: (0,) * len(shape))
    grid_spec = pltpu.PrefetchScalarGridSpec(
        num_scalar_prefetch=1,
        grid=(_W_IN_PACKED // LANES,),
        in_specs=[full((T, D_MODEL)), full((1, D_MODEL)), full((T, D_MODEL)), full((T, D_MODEL)),
                  full((T, LANES)), full((T, LANES)),
                  pl.BlockSpec((D_MODEL, LANES), lambda n, f: (0, n))],
        out_specs=pl.BlockSpec((T, LANES), lambda n, f: (0, n)),
    )
    return pl.pallas_call(
        _in_s_body, grid_spec=grid_spec,
        out_shape=jax.ShapeDtypeStruct((T, _W_IN_PACKED), F32),
        name="in_proj_sample",
    )(jnp.asarray(_IN_ROPE_FLAGS), x, g.reshape(1, D_MODEL), sc, sh, cos_t, sin_t, w_packed)


def _mix_s_body(x_ref, g_ref, sc_ref, sh_ref, ya_ref, yb_ref, yc_ref, wg0, wg1, wg2, bg0, bg1, bg2, wbr_ref, o_ref):
    h = _norm_mod(x_ref[...], g_ref[...], sc_ref[...], sh_ref[...])
    mix = None
    for n, (y_ref, wg, bg) in enumerate(((ya_ref, wg0, bg0), (yb_ref, wg1, bg1), (yc_ref, wg2, bg2))):
        gate = jax.nn.sigmoid(_dot_hp(h, wg[...]) + bg[...])
        proj = _dot_hp(y_ref[...], wbr_ref[n])
        mix = gate * proj if mix is None else mix + gate * proj
    o_ref[...] = mix


def _gate_mix_sample(x, g, sc, sh, ya, yb, yc, w_bgate, b_bgate, w_branch, tn=256):
    T = x.shape[0]
    nj = D_MODEL // tn
    full = lambda shape: pl.BlockSpec(shape, lambda j: (0,) * len(shape))
    wg = [pl.BlockSpec((D_MODEL, tn), lambda j, n=n: (0, n * nj + j)) for n in range(N_BRANCH)]
    bg = [pl.BlockSpec((1, tn), lambda j, n=n: (0, n * nj + j)) for n in range(N_BRANCH)]
    bb = b_bgate.reshape(1, N_BRANCH * D_MODEL)
    return pl.pallas_call(
        _mix_s_body, grid=(nj,),
        in_specs=[full((T, D_MODEL)), full((1, D_MODEL)), full((T, D_MODEL)), full((T, D_MODEL)),
                  full((T, BRANCH_W)), full((T, BRANCH_W)), full((T, BRANCH_W))] + wg + bg
                 + [pl.BlockSpec((N_BRANCH, BRANCH_W, tn), lambda j: (0, 0, j))],
        out_specs=pl.BlockSpec((T, tn), lambda j: (0, j)),
        out_shape=jax.ShapeDtypeStruct((T, D_MODEL), F32),
        name="gate_mix_sample",
    )(x, g.reshape(1, D_MODEL), sc, sh, ya, yb, yc, w_bgate, w_bgate, w_bgate, bb, bb, bb, w_branch)


def _out_s_body(x_ref, mix_ref, g1_ref, wout_ref, gf_ref, sc2_ref, sh2_ref, wr_ref, br_ref, xo_ref, h2_ref, lg_ref):
    xn = x_ref[...] + g1_ref[...] * _dot_hp(mix_ref[...], wout_ref[...])
    xo_ref[...] = xn
    h2 = _norm_mod(xn, gf_ref[...], sc2_ref[...], sh2_ref[...])
    h2_ref[...] = h2
    lg_ref[...] = _dot_hp(h2, wr_ref[...]) + br_ref[...]


def _out_sample(x, mix, g1, w_out, gf, sc2, sh2, wr, br):
    T = x.shape[0]
    return pl.pallas_call(
        _out_s_body,
        out_shape=[jax.ShapeDtypeStruct((T, D_MODEL), F32), jax.ShapeDtypeStruct((T, D_MODEL), F32),
                   jax.ShapeDtypeStruct((T, LANES), F32)],
        compiler_params=pltpu.CompilerParams(vmem_limit_bytes=VMEM_LIMIT),
        name="out_sample",
    )(x, mix, g1, w_out, gf.reshape(1, D_MODEL), sc2, sh2, wr, br)


def _moe_s_body(h_ref, gate_ref, wu_ref, bu_ref, wd_ref, bd_ref, x_ref, g2_ref, o_ref, acc):
    e = pl.program_id(0)

    @pl.when(e == 0)
    def _():
        acc[...] = jnp.zeros_like(acc)

    uu = _dot_hp(h_ref[...], wu_ref[0]) + bu_ref[0]
    gl = jnp.minimum(uu[:, :D_FF], SWIGLU_LIMIT)
    up = jnp.clip(uu[:, D_FF:], -SWIGLU_LIMIT, SWIGLU_LIMIT)
    act = gl * jax.nn.sigmoid(SWIGLU_ALPHA * gl) * (up + 1.0)
    acc[...] += gate_ref[0] * (_dot_hp(act, wd_ref[0]) + bd_ref[0])

    @pl.when(e == N_EXPERTS - 1)
    def _():
        o_ref[...] = x_ref[...] + g2_ref[...] * acc[...]


def _moe_sample(h2, logits, w_up, b_up, w_down, b_down, x, g2):
    T = h2.shape[0]
    top_v, top_i = lax.top_k(logits, TOP_K)
    wts = jax.nn.softmax(top_v, axis=-1)
    gate = jnp.einsum('tk,tke->et', wts, jax.nn.one_hot(top_i, N_EXPERTS, dtype=F32),
                      precision=lax.Precision.HIGHEST)[:, :, None]
    full = lambda shape: pl.BlockSpec(shape, lambda e: (0,) * len(shape))
    return pl.pallas_call(
        _moe_s_body, grid=(N_EXPERTS,),
        in_specs=[full((T, D_MODEL)), pl.BlockSpec((1, T, 1), lambda e: (e, 0, 0)),
                  pl.BlockSpec((1, D_MODEL, 2 * D_FF), lambda e: (e, 0, 0)),
                  pl.BlockSpec((1, 1, 2 * D_FF), lambda e: (e, 0, 0)),
                  pl.BlockSpec((1, D_FF, D_MODEL), lambda e: (e, 0, 0)),
                  pl.BlockSpec((1, 1, D_MODEL), lambda e: (e, 0, 0)),
                  full((T, D_MODEL)), full((T, D_MODEL))],
        out_specs=full((T, D_MODEL)),
        out_shape=jax.ShapeDtypeStruct((T, D_MODEL), F32),
        scratch_shapes=[pltpu.VMEM((T, D_MODEL), F32)],
        compiler_params=pltpu.CompilerParams(vmem_limit_bytes=VMEM_LIMIT, dimension_semantics=("arbitrary",)),
        name="moe_sample",
    )(h2, gate, w_up, b_up.reshape(N_EXPERTS, 1, 2 * D_FF), w_down, b_down.reshape(N_EXPERTS, 1, D_MODEL), x, g2)


def _final_norm_body(x_ref, g_ref, o_ref):
    x = x_ref[...]
    o_ref[...] = x * lax.rsqrt(jnp.mean(x * x, axis=-1, keepdims=True) + EPS) * g_ref[...]


def _final_norm(x, g, tm):
    T = x.shape[0]
    return pl.pallas_call(
        _final_norm_body,
        grid=(T // tm,),
        in_specs=[pl.BlockSpec((tm, D_MODEL), lambda i: (i, 0)), pl.BlockSpec((1, D_MODEL), lambda i: (0, 0))],
        out_specs=pl.BlockSpec((tm, D_MODEL), lambda i: (i, 0)),
        out_shape=jax.ShapeDtypeStruct((T, D_MODEL), F32),
        name="final_norm",
    )(x, g.reshape(1, D_MODEL))


def _rms_norm(x, g):
    y = x * lax.rsqrt(jnp.mean(x * x, axis=-1, keepdims=True) + EPS)
    return y * g


def _masked_softmax(s, mask):
    s = jnp.where(mask, s, -1e30)
    m = jnp.max(s, axis=-1, keepdims=True)
    e = jnp.where(mask, jnp.exp(s - m), 0.0)
    return e / jnp.maximum(jnp.sum(e, axis=-1, keepdims=True), 1e-30)


def _sweep_queries(fn, qs, qpos, block):
    B, Q = qs[0].shape[:2]
    nb = Q // block

    def split(a):
        return jnp.moveaxis(a.reshape((B, nb, block) + a.shape[2:]), 1, 0)

    out = lax.map(lambda xs: fn(*xs[0], xs[1]), (tuple(split(a) for a in qs), qpos.reshape(nb, block)))
    out = jnp.moveaxis(out, 0, 1)
    return out.reshape((B, Q) + out.shape[3:])


def _diff_attn_core(q, k, v, qpos, kpos, lam, lam_init, g_sub):
    s = jnp.einsum('bqhcd,bkhcd->bhcqk', q, k) * (DA_QK ** -0.5)
    mask = kpos[None, :] <= qpos[:, None]
    p = _masked_softmax(s, mask)
    a = p[:, :, 0] - lam * p[:, :, 1]
    o = jnp.einsum('bhqk,bkhd->bqhd', a, v)
    return _rms_norm(o, g_sub) * (1.0 - lam_init)


def _mlstm_chunked(q, k, v, log_i, log_f, C0, n0, m0):
    B, S, H, DK = q.shape
    L = math.gcd(S, ML_CHUNK)
    nc = S // L

    def to_chunks(a):
        return jnp.moveaxis(a.reshape((B, nc, L) + a.shape[2:]), 1, 0)

    kf = k * (DK ** -0.5)
    causal = jnp.tril(jnp.ones((L, L), dtype=bool))[None, :, :, None]

    def step(carry, xs):
        C, n, m = carry
        qc, kc, vc, ic, fc = xs
        b = jnp.cumsum(fc, axis=1)
        a = b + m[:, None, :]
        D = jnp.where(causal, b[:, :, None, :] - b[:, None, :, :] + ic[:, None, :, :], -jnp.inf)
        mt = jnp.maximum(a, jnp.max(D, axis=2))
        w_inter = jnp.exp(a - mt)
        w_intra = jnp.exp(D - mt[:, :, None, :])
        sc = w_intra * jnp.einsum('bthd,bshd->btsh', qc, kc)
        num = w_inter[..., None] * jnp.einsum('bhvd,bthd->bthv', C, qc) + jnp.einsum('btsh,bshv->bthv', sc, vc)
        den = w_inter * jnp.einsum('bhd,bthd->bth', n, qc) + jnp.sum(sc, axis=2)
        h = num / jnp.maximum(jnp.abs(den), jnp.exp(-mt))[..., None]
        m_new = mt[:, -1]
        wi = w_intra[:, -1]
        wc = w_inter[:, -1]
        C_new = wc[..., None, None] * C + jnp.einsum('bsh,bshv,bshd->bhvd', wi, vc, kc)
        n_new = wc[..., None] * n + jnp.einsum('bsh,bshd->bhd', wi, kc)
        return (C_new, n_new, m_new), h

    (C1, n1, m1), hs = lax.scan(step, (C0, n0, m0),
                                (to_chunks(q), to_chunks(kf), to_chunks(v), to_chunks(log_i), to_chunks(log_f)))
    hs = jnp.moveaxis(hs, 0, 1).reshape(B, S, H, v.shape[-1])
    return hs, C1, n1, m1


def _nsa_compress(rows, pos_emb, w):
    B, T, G, dh = rows.shape
    Tp = -(-T // NSA_SEL_LEN) * NSA_SEL_LEN
    rows = jnp.pad(rows, ((0, 0), (0, Tp - T), (0, 0), (0, 0)))
    ch = rows.reshape(B, Tp // NSA_CMP_STRIDE, NSA_CMP_STRIDE, G, dh)
    n_sub = NSA_CMP_LEN // NSA_CMP_STRIDE
    nc = ch.shape[1] - n_sub + 1
    blocks = jnp.concatenate([ch[:, i:i + nc] for i in range(n_sub)], axis=2)
    blocks = blocks + pos_emb[None, None, :, None, :]
    return jnp.einsum('bnlgd,lde->bnge', blocks, w.reshape(NSA_CMP_LEN, dh, dh))


def _nsa_cmp_branch(q, kc, vc, qpos):
    B, Q, H, dh = q.shape
    nc = kc.shape[1]
    qg = q.reshape(B, Q, NSA_GROUPS, NSA_HPG, dh)
    s = jnp.einsum('bqgjd,bngd->bgjqn', qg, kc) * (dh ** -0.5)
    ends = jnp.arange(nc) * NSA_CMP_STRIDE + NSA_CMP_LEN - 1
    mask = ends[None, :] <= qpos[:, None]
    p = _masked_softmax(s, mask)
    o = jnp.einsum('bgjqn,bngd->bqgjd', p, vc).reshape(B, Q, H, dh)
    return o, jnp.sum(p, axis=2)


def _nsa_select(imp_cmp, qpos, nsb):
    r = NSA_SEL_LEN // NSA_CMP_STRIDE
    imp = jnp.pad(imp_cmp, ((0, 0), (0, 0), (0, 0), (0, nsb * r - imp_cmp.shape[-1])))
    imp = jnp.sum(imp.reshape(imp.shape[:3] + (nsb, r)), axis=-1)
    j = jnp.arange(nsb)[None, :]
    qb = (qpos // NSA_SEL_LEN)[:, None]
    forced = (j == 0) | (j == qb) | (j == qb - 1)
    score = jnp.where(forced, FORCE_SCORE, imp)
    score = jnp.where(j <= qb, score, -FORCE_SCORE)
    _, idx = lax.top_k(score, min(NSA_TOPK, nsb))
    valid = idx <= (qpos // NSA_SEL_LEN)[None, None, :, None]
    return jnp.transpose(idx, (0, 2, 1, 3)), jnp.transpose(valid, (0, 2, 1, 3))


def _to_sel_blocks(rows, nsb):
    B, T, G, dh = rows.shape
    rows = jnp.pad(rows, ((0, 0), (0, nsb * NSA_SEL_LEN - T), (0, 0), (0, 0)))
    return jnp.transpose(rows.reshape(B, nsb, NSA_SEL_LEN, G, dh), (0, 3, 1, 2, 4))


def _nsa_slc_branch(q, idx, valid, qpos, kblk, vblk):
    B, Qb, H, dh = q.shape
    bi = jnp.arange(B)[:, None, None, None]
    gi = jnp.arange(NSA_GROUPS)[None, None, :, None]
    ks = kblk[bi, gi, idx]
    vs = vblk[bi, gi, idx]
    kpos = idx[..., None] * NSA_SEL_LEN + jnp.arange(NSA_SEL_LEN)
    mask = valid[..., None] & (kpos <= qpos[None, :, None, None, None])
    qg = q.reshape(B, Qb, NSA_GROUPS, NSA_HPG, dh)
    s = jnp.einsum('bqgjd,bqgnld->bqgjnl', qg, ks) * (dh ** -0.5)
    p = _masked_softmax(s.reshape(B, Qb, NSA_GROUPS, NSA_HPG, -1), mask.reshape(B, Qb, NSA_GROUPS, 1, -1))
    o = jnp.einsum('bqgjm,bqgmd->bqgjd', p, vs.reshape(B, Qb, NSA_GROUPS, -1, dh))
    return o.reshape(B, Qb, H, dh)


def _nsa_win_branch(q, kw, vw, qpos, kpos):
    B, NB, QB, H, dh = q.shape
    qg = q.reshape(B, NB, QB, NSA_GROUPS, NSA_HPG, dh)
    s = jnp.einsum('bnqgjd,bnkgd->bngjqk', qg, kw) * (dh ** -0.5)
    dpos = qpos[:, :, None] - kpos[:, None, :]
    mask = (dpos >= 0) & (dpos < NSA_WINDOW) & (kpos[:, None, :] >= 0)
    p = _masked_softmax(s, mask[None, :, None, None])
    o = jnp.einsum('bngjqk,bnkgd->bnqgjd', p, vw)
    return o.reshape(B, NB * QB, H, dh)


def _gather_pages(cache, l, page_table):
    g = cache[l, page_table]
    return g.reshape((g.shape[0], g.shape[1] * g.shape[2]) + g.shape[3:])


def _mlstm_mixer(oml, osm, gate_b, norm_g, B, Q, past):
    qm = oml[:, 0:256].reshape(B, Q, ML_HEADS, ML_QK)
    km = oml[:, 256:512].reshape(B, Q, ML_HEADS, ML_QK)
    vm = oml[:, 512:1024].reshape(B, Q, ML_HEADS, ML_V)
    om = oml[:, 1024:1536].reshape(B, Q, ML_HEADS, ML_V)
    im = osm[:, 0:4].reshape(B, Q, ML_HEADS)
    fm = osm[:, 4:8].reshape(B, Q, ML_HEADS)
    log_i = im + gate_b[:ML_HEADS]
    log_f = jax.nn.log_sigmoid(fm + gate_b[ML_HEADS:])
    if past is None:
        C0 = jnp.zeros((B, ML_HEADS, ML_V, ML_QK), F32)
        n0 = jnp.zeros((B, ML_HEADS, ML_QK), F32)
        m0 = jnp.zeros((B, ML_HEADS), F32)
    else:
        C0, n0, m0 = past
    hm, C1, n1, m1 = _mlstm_chunked(qm, km, vm, log_i, log_f, C0, n0, m0)
    hm = _rms_norm(hm, norm_g) * jax.nn.sigmoid(om)
    return hm.reshape(B * Q, ML_HEADS * ML_V), C1, n1, m1


def _nsa_mixer(oq, onkv, owin, osm, pos, cmp_pos, cmp_w, B, Q, past):
    qn_rot = oq[:, 512:1024].reshape(B, Q, NSA_HEADS, NSA_DH)
    qn = oq[:, 1024:1536].reshape(B, Q, NSA_HEADS, NSA_DH)
    nk = onkv.reshape(B, Q, 4, NSA_GROUPS, NSA_DH)
    cmp_k, cmp_v, slc_k, slc_v = nk[:, :, 0], nk[:, :, 1], nk[:, :, 2], nk[:, :, 3]
    wk = owin.reshape(B, Q, 2, NSA_GROUPS, NSA_DH)
    win_k, win_v = wk[:, :, 0], wk[:, :, 1]
    if past is not None:
        pn, buf = past
        cmp_k = jnp.concatenate([pn[:, :, 0], cmp_k], axis=1)
        cmp_v = jnp.concatenate([pn[:, :, 1], cmp_v], axis=1)
        slc_k = jnp.concatenate([pn[:, :, 2], slc_k], axis=1)
        slc_v = jnp.concatenate([pn[:, :, 3], slc_v], axis=1)
    nsb = -(-cmp_k.shape[1] // NSA_SEL_LEN)
    kc = _nsa_compress(cmp_k, cmp_pos[0], cmp_w[0])
    vc = _nsa_compress(cmp_v, cmp_pos[1], cmp_w[1])
    o_cmp, imp = _nsa_cmp_branch(qn, kc, vc, pos)
    idx, valid = _nsa_select(imp, pos, nsb)
    kblk = _to_sel_blocks(slc_k, nsb)
    vblk = _to_sel_blocks(slc_v, nsb)
    o_slc = _sweep_queries(lambda q, ix, ok, p: _nsa_slc_branch(q, ix, ok, p, kblk, vblk),
                           (qn_rot, idx, valid), pos, math.gcd(Q, NSA_QBLOCK))
    if past is None:
        win_len = min(NSA_WINDOW, Q)
        QB = math.gcd(Q, NSA_QBLOCK)
        nb, nw = Q // QB, NSA_WINDOW // QB
        kp = jnp.pad(win_k, ((0, 0), (NSA_WINDOW, 0), (0, 0), (0, 0))).reshape(B, nb + nw, QB, NSA_GROUPS, NSA_DH)
        vp = jnp.pad(win_v, ((0, 0), (NSA_WINDOW, 0), (0, 0), (0, 0))).reshape(B, nb + nw, QB, NSA_GROUPS, NSA_DH)
        band_k = jnp.concatenate([kp[:, i:i + nb] for i in range(nw + 1)], axis=2)
        band_v = jnp.concatenate([vp[:, i:i + nb] for i in range(nw + 1)], axis=2)
        kpos_w = (jnp.arange(nb) * QB)[:, None] - NSA_WINDOW + jnp.arange((nw + 1) * QB)[None, :]
        o_win = _nsa_win_branch(qn_rot.reshape(B, nb, QB, NSA_HEADS, NSA_DH), band_k, band_v,
                                pos.reshape(nb, QB), kpos_w)
        new_win = jnp.stack([win_k, win_v], axis=2)[:, Q - win_len:]
    else:
        win_len = buf.shape[1]
        kw = jnp.concatenate([buf[:, :, 0], win_k], axis=1)
        vw = jnp.concatenate([buf[:, :, 1], win_v], axis=1)
        kpos_w = (pos[0] - win_len) + jnp.arange(win_len + Q)
        o_win = _nsa_win_branch(qn_rot[:, None], kw[:, None], vw[:, None], pos[None, :], kpos_w[None, :])
        new_win = jnp.stack([kw, vw], axis=2)[:, -win_len:]
    gn = jax.nn.sigmoid(osm[:, 8:32].reshape(B, Q, NSA_HEADS, 3))
    yc = (gn[..., 0:1] * o_cmp + gn[..., 1:2] * o_slc + gn[..., 2:3] * o_win).reshape(B * Q, NSA_HEADS * NSA_DH)
    return yc, new_win


def _nsa_sample(oq, onkv, owin, osm, cache_nsa_kv, page_table, l, buf, cmp_pos, cmp_w):
    Bs, n_pages = page_table.shape
    page = cache_nsa_kv.shape[2]
    pos = n_pages * page
    G, HPG, dh = NSA_GROUPS, NSA_HPG, NSA_DH
    qn_rot = oq[:, 512:1024].reshape(Bs, G, HPG, dh)
    qn = oq[:, 1024:1536].reshape(Bs, G, HPG, dh)
    new = onkv.reshape(Bs, 4, G, dh)
    wnew = owin.reshape(Bs, 2, G, dh)
    scale = dh ** -0.5
    n_valid = (pos - (NSA_CMP_LEN - 1)) // NSA_CMP_STRIDE + 1
    n_ch = pos // NSA_CMP_STRIDE
    assert n_valid + 1 <= n_ch
    x = cache_nsa_kv[l][:, :, 0:2][page_table].reshape(Bs, n_ch, NSA_CMP_STRIDE, 2, G, dh)
    w = cmp_w.reshape(2, NSA_CMP_LEN, dh, dh)
    first = jnp.einsum('bcjkgd,kjde->bckge', x, w[:, :NSA_CMP_STRIDE])
    second = jnp.einsum('bcjkgd,kjde->bckge', x, w[:, NSA_CMP_STRIDE:])
    const = jnp.einsum('kld,klde->ke', cmp_pos, w)
    kvc = first[:, :n_valid] + second[:, 1:n_valid + 1] + const[None, None, :, None, :]
    kc, vc = kvc[:, :, 0], kvc[:, :, 1]
    p = jax.nn.softmax(jnp.einsum('bgjd,bngd->bgjn', qn, kc) * scale, axis=-1)
    o_cmp = jnp.einsum('bgjn,bngd->bgjd', p, vc)
    nsb = -(-(pos + 1) // NSA_SEL_LEN)
    r = NSA_SEL_LEN // NSA_CMP_STRIDE
    imp = jnp.pad(jnp.sum(p, axis=2), ((0, 0), (0, 0), (0, nsb * r - n_valid)))
    imp = jnp.sum(imp.reshape(Bs, G, nsb, r), axis=-1)
    qb = pos // NSA_SEL_LEN
    jj = jnp.arange(nsb)
    score = jnp.where((jj == 0) | (jj == qb) | (jj == qb - 1), FORCE_SCORE, imp)
    score = jnp.where(jj <= qb, score, -FORCE_SCORE)
    _, idx = lax.top_k(score, min(NSA_TOPK, nsb))
    n_past_blk = pos // NSA_SEL_LEN
    assert qb == n_past_blk and page % NSA_SEL_LEN == 0
    per_page = page // NSA_SEL_LEN
    idx_c = jnp.minimum(idx, n_past_blk - 1)
    pg = jnp.take_along_axis(page_table[:, None, :], idx_c // per_page, axis=2)
    src = cache_nsa_kv[l].reshape(cache_nsa_kv.shape[1] * per_page, NSA_SEL_LEN, 4, G, dh)
    blk = (pg * per_page + idx_c % per_page)[..., None]
    tok = jnp.arange(NSA_SEL_LEN)[None, None, None, :]
    gi = jnp.arange(G)[None, :, None, None]
    ks = src[blk, tok, 2, gi]
    vs = src[blk, tok, 3, gi]
    s_past = jnp.einsum('bgjd,bgnld->bgjnl', qn_rot, ks) * scale
    s_past = jnp.where((idx < n_past_blk)[:, :, None, :, None], s_past, -1e30).reshape(Bs, G, HPG, -1)
    s_new = jnp.einsum('bgjd,bgd->bgj', qn_rot, new[:, 2]) * scale
    s_all = jnp.concatenate([s_past, s_new[..., None]], axis=-1)
    p_all = jax.nn.softmax(s_all, axis=-1)
    o_slc = (jnp.einsum('bgjm,bgmd->bgjd', p_all[..., :-1], vs.reshape(Bs, G, -1, dh))
             + p_all[..., -1:] * new[:, 3][:, :, None, :])
    win_len = buf.shape[1]
    kw = jnp.concatenate([buf[:, :, 0], wnew[:, None, 0]], axis=1)
    vw = jnp.concatenate([buf[:, :, 1], wnew[:, None, 1]], axis=1)
    dpos = win_len - jnp.arange(win_len + 1)
    s_w = jnp.einsum('bgjd,bkgd->bgjk', qn_rot, kw) * scale
    p_w = jax.nn.softmax(jnp.where(dpos < NSA_WINDOW, s_w, -1e30), axis=-1)
    o_win = jnp.einsum('bgjk,bkgd->bgjd', p_w, vw)
    new_win = jnp.stack([kw, vw], axis=2)[:, -win_len:]
    gn = jax.nn.sigmoid(osm[:, 8:32].reshape(Bs, G, HPG, 3))
    yc = gn[..., 0:1] * o_cmp + gn[..., 1:2] * o_slc + gn[..., 2:3] * o_win
    return yc.reshape(Bs, NSA_HEADS * dh), new_win


def _diff_attn_sample(oq, odkv, pd, pos, da_lam, g_sub, lam_init, B, Q):
    qa = oq[:, 0:512].reshape(B, Q, DA_HEADS, 2, DA_QK)
    ka = odkv[:, 0:512].reshape(B, Q, DA_HEADS, 2, DA_QK)
    va = odkv[:, 512:1024].reshape(B, Q, DA_HEADS, DA_V)
    k_all = jnp.concatenate([pd[:, :, 0].reshape(B, -1, DA_HEADS, 2, DA_QK), ka], axis=1)
    v_all = jnp.concatenate([pd[:, :, 1], va], axis=1)
    kpos = jnp.arange(k_all.shape[1])
    lam = jnp.exp(jnp.sum(da_lam[0] * da_lam[1])) - jnp.exp(jnp.sum(da_lam[2] * da_lam[3])) + lam_init
    ya = _diff_attn_core(qa, k_all, v_all, pos, kpos, lam, lam_init, g_sub)
    return ya.reshape(B * Q, DA_HEADS * DA_V)


def kernel(x_prompt, x_sample, cache_diff_kv, cache_nsa_kv, state_nsa_win, state_mlstm_C, state_mlstm_n, state_mlstm_m, page_table, c_prompt, c_sample, norm_mix_g, norm_ffn_g, w_ada, b_ada, w_in, da_lam, da_subln_g, ml_gate_b, ml_norm_g, nsa_cmp_pos, nsa_cmp_w, w_branch, w_bgate, b_bgate, w_out, w_router, b_router, w_up, b_up, w_down, b_down, final_g):
    B, S, D = x_prompt.shape
    Bs, Qs, _ = x_sample.shape
    Tp, Ts = B * S, Bs * Qs
    tm_p = 256
    pos_p = jnp.arange(S)
    pos_s = PAST_LEN + jnp.arange(Qs)
    cos_p, sin_p = _rope_tables(pos_p)
    cos_s, sin_s = _rope_tables(jnp.broadcast_to(pos_s[None, :], (Bs, Qs)).reshape(Ts))

    mods = _ada_all(jnp.concatenate([c_prompt, c_sample], axis=0), w_ada, b_ada)

    xp = x_prompt.reshape(Tp, D)
    xs = x_sample.reshape(Ts, D)
    sp = {k: [] for k in ('diff', 'nsa', 'win', 'C', 'n', 'm')}
    ss = {k: [] for k in ('diff', 'nsa', 'win', 'C', 'n', 'm')}
    for l in range(DEPTH):
        lam_init = 0.8 - 0.6 * math.exp(-0.3 * l)
        w_packed = _pack_w_in_ext(w_in[l])
        wbg = w_bgate[l].astype(BF16)
        wbr = w_branch[l].astype(BF16)
        wout = w_out[l].astype(BF16)
        wr = jnp.pad(w_router[l], ((0, 0), (0, LANES - N_EXPERTS)))
        wrh = wr.astype(BF16)
        wrl = (wr - wrh.astype(F32)).astype(BF16)
        br = jnp.pad(b_router[l], (0, LANES - N_EXPERTS)).reshape(1, LANES)
        mod_p = [m.reshape(B, 1, D) for m in jnp.split(mods[l, :B], 6, axis=-1)]
        mod_s = [m.reshape(1, Ts, D) for m in jnp.split(mods[l, B:], 6, axis=-1)]

        oq, odkv, onkv, owin, oml, osm, odkv_bf, onsa_bf = _in_proj(xp, norm_mix_g[l], mod_p[1], mod_p[0], cos_p, sin_p,
                                                           w_packed, tm_p, S)
        ya = _diff_attn_prompt(oq, odkv_bf, da_lam[l], da_subln_g[l], lam_init, B, S)
        yb, C1, n1, m1 = _mlstm_prompt(oml, osm, ml_gate_b[l], ml_norm_g[l], B, S)
        kc = _nsa_compress_prompt(onkv, nsa_cmp_pos[l], nsa_cmp_w[l], B, S)
        yc = _nsa_prompt(oq, onsa_bf, osm, kc, B, S)
        new_win = owin.reshape(B, S, 2, NSA_GROUPS, NSA_DH)[:, S - min(NSA_WINDOW, S):]
        xp, h2p, lgp = _merge(xp, ya, yb, yc, norm_mix_g[l], (mod_p[1], mod_p[0], mod_p[2], mod_p[4], mod_p[3]),
                              norm_ffn_g[l], wbg, b_bgate[l], wbr, wout, wrh, wrl, br, tm_p, S)
        sp['diff'].append(odkv.reshape(B, S, 2, DA_HEADS, DA_V))
        sp['nsa'].append(onkv.reshape(B, S, 4, NSA_GROUPS, NSA_DH))
        sp['win'].append(new_win)
        sp['C'].append(C1); sp['n'].append(n1); sp['m'].append(m1)

        sc1s, sh1s = mod_s[1].reshape(Ts, D), mod_s[0].reshape(Ts, D)
        u = _in_proj_sample(xs, norm_mix_g[l], sc1s, sh1s, cos_s, sin_s, _pack_w_in(w_in[l], F32))
        oq, odkv, onkv, owin, oml, osm = jnp.split(u, [int(c) for c in np.cumsum(_IN_OUT_W)[:-1]], axis=1)
        ya = _diff_attn_sample_paged(oq, odkv, cache_diff_kv, page_table, l, da_lam[l], da_subln_g[l], lam_init)
        with jax.default_matmul_precision("highest"):
            yb, C1, n1, m1 = _mlstm_mixer(oml, osm, ml_gate_b[l], ml_norm_g[l], Bs, Qs,
                                          (state_mlstm_C[l], state_mlstm_n[l], state_mlstm_m[l]))
            yc, new_win = _nsa_sample(oq, onkv, owin, osm, cache_nsa_kv, page_table, l, state_nsa_win[l],
                                      nsa_cmp_pos[l], nsa_cmp_w[l])
        mix = _gate_mix_sample(xs, norm_mix_g[l], sc1s, sh1s, ya, yb, yc, w_bgate[l], b_bgate[l], w_branch[l])
        xs, h2s, lgs = _out_sample(xs, mix, mod_s[2].reshape(Ts, D), w_out[l], norm_ffn_g[l],
                                   mod_s[4].reshape(Ts, D), mod_s[3].reshape(Ts, D), wr, br)
        xs = _moe_sample(h2s, lgs[:, :N_EXPERTS], w_up[l], b_up[l], w_down[l], b_down[l], xs,
                         mod_s[5].reshape(Ts, D))
        ss['diff'].append(odkv.reshape(Bs, Qs, 2, DA_HEADS, DA_V))
        ss['nsa'].append(onkv.reshape(Bs, Qs, 4, NSA_GROUPS, NSA_DH))
        ss['win'].append(new_win)
        ss['C'].append(C1); ss['n'].append(n1); ss['m'].append(m1)

        ym = _moe(h2p, lgp[:, :N_EXPERTS], w_up[l], b_up[l], w_down[l], b_down[l])
        xp = xp + jnp.repeat(mod_p[5].reshape(B, D), S, axis=0) * ym

    y_prompt = _final_norm(xp, final_g, tm_p).reshape(B, S, D)
    y_sample = _final_norm(xs, final_g, Ts).reshape(Bs, Qs, D)
    stk = lambda d, k: jnp.stack(d[k], axis=0)
    return (y_prompt, y_sample,
            stk(sp, 'diff'), stk(sp, 'nsa'), stk(sp, 'win'), stk(sp, 'C'), stk(sp, 'n'), stk(sp, 'm'),
            stk(ss, 'diff'), stk(ss, 'nsa'), stk(ss, 'win'), stk(ss, 'C'), stk(ss, 'n'), stk(ss, 'm'))
```
